```python
import math
import jax, jax.numpy as jnp
from jax import lax
import numpy as np

D_MODEL = 1024
BATCH = 8
SEQ = 4096
DEPTH = 2

HEAD_DIM = 64
N_HEADS = D_MODEL // HEAD_DIM
N_HEADS_A = N_HEADS // 4
N_HEADS_B = N_HEADS // 4
N_HEADS_C = N_HEADS - N_HEADS_A - N_HEADS_B
WIDTH_A = N_HEADS_A * HEAD_DIM
WIDTH_B = N_HEADS_B * HEAD_DIM
WIDTH_C = N_HEADS_C * HEAD_DIM
DIFF_DIM = HEAD_DIM // 2
ROPE_THETA = 500000.0
ROPE_FRACTION = 4
Q_BLOCK = 128
MOBA_BLOCK = 256
MOBA_TOPK = 3
MOBA_Q_CHUNK = 32
DILATED_PATTERNS = ((128, 1), (512, 4), (2048, 16))
SW_BLOCK = 128
D_FF = ((8 * D_MODEL // 3 + 255) // 256) * 256
CONV_WIDTH = 3
EPS = 1e-6
NEG_INF = -1e30

kernel_name = 'hybrid_diff_moba_dilated_block'


def rms_norm(x, g):
    xf = x.astype(jnp.float32)
    y = xf * lax.rsqrt(jnp.mean(xf * xf, axis=-1, keepdims=True) + EPS)
    return (y * g.astype(jnp.float32)).astype(x.dtype)


def partial_rope(x, positions):
    d = x.shape[-1]
    rot = d // ROPE_FRACTION
    half = rot // 2
    inv = jnp.power(jnp.float32(ROPE_THETA), -jnp.arange(half, dtype=jnp.float32) * 2.0 / rot)
    ang = positions.astype(jnp.float32)[..., None] * inv
    ang = ang.reshape((x.shape[0],) + (1,) * (x.ndim - 3) + ang.shape[1:])
    cos = jnp.cos(ang).astype(x.dtype)
    sin = jnp.sin(ang).astype(x.dtype)
    x1 = x[..., :half]
    x2 = x[..., half:rot]
    return jnp.concatenate([x1 * cos - x2 * sin, x2 * cos + x1 * sin, x[..., rot:]], axis=-1)


def to_heads(t, h):
    b, s, _ = t.shape
    return t.reshape(b, s, h, -1).transpose(0, 2, 1, 3)


def from_heads(t):
    b, h, s, d = t.shape
    return t.transpose(0, 2, 1, 3).reshape(b, s, h * d)


def differential_attention(q, k, v, positions, lam, g_head, lam_init):
    b, s, _ = q.shape
    h = N_HEADS_A
    q = partial_rope(q.reshape(b, s, h, 2, DIFF_DIM).transpose(0, 2, 3, 1, 4), positions)
    k = partial_rope(k.reshape(b, s, h, 2, DIFF_DIM).transpose(0, 2, 3, 1, 4), positions)
    vf = to_heads(v, h).astype(jnp.float32)
    nq = s // Q_BLOCK
    qb = q.reshape(b, h, 2, nq, Q_BLOCK, DIFF_DIM).transpose(3, 0, 1, 2, 4, 5)
    key_pos = jnp.arange(s)
    scale = DIFF_DIM ** -0.5

    def block(args):
        qi, i = args
        sc = jnp.einsum('bhcqd,bhckd->bhcqk', qi, k).astype(jnp.float32) * scale
        q_pos = i * Q_BLOCK + jnp.arange(Q_BLOCK)
        sc = jnp.where(key_pos[None, :] <= q_pos[:, None], sc, NEG_INF)
        p = jax.nn.softmax(sc, axis=-1)
        a = p[:, :, 0] - lam * p[:, :, 1]
        return jnp.einsum('bhqk,bhkd->bhqd', a, vf)

    o = lax.map(block, (qb, jnp.arange(nq)))
    o = o.transpose(1, 2, 0, 3, 4).reshape(b, h, s, HEAD_DIM)
    o = rms_norm(o, g_head) * (1.0 - lam_init)
    return from_heads(o)


def moba_attention(q, k, v, positions):
    b, s, _ = q.shape
    h = N_HEADS_B
    q = partial_rope(to_heads(q, h), positions)
    k = partial_rope(to_heads(k, h), positions)
    v = to_heads(v, h)
    nb = -(-s // MOBA_BLOCK)
    sp = nb * MOBA_BLOCK
    pad = ((0, 0), (0, 0), (0, sp - s), (0, 0))
    q, k, v = jnp.pad(q, pad), jnp.pad(k, pad), jnp.pad(v, pad)
    n_sel = max(1, min(MOBA_TOPK, nb - 1))
    kb = k.reshape(b, h, nb, MOBA_BLOCK, HEAD_DIM)
    vb = v.reshape(b, h, nb, MOBA_BLOCK, HEAD_DIM)
    k_mean = jnp.mean(kb.astype(jnp.float32), axis=3)
    gate = jnp.einsum('bhsd,bhnd->bhsn', q.astype(jnp.float32), k_mean)
    q_blk = jnp.arange(sp) // MOBA_BLOCK
    gate = jnp.where(jnp.arange(nb)[None, :] < q_blk[:, None], gate, NEG_INF)
    _, sel = lax.top_k(gate, n_sel)
    valid = jnp.arange(n_sel)[None, :] < q_blk[:, None]
    c = MOBA_Q_CHUNK
    nc = sp // c
    qc = q.reshape(b, h, nc, c, HEAD_DIM).transpose(2, 0, 1, 3, 4)
    selc = sel.reshape(b, h, nc, c, n_sel).transpose(2, 0, 1, 3, 4)
    validc = valid.reshape(nc, c, n_sel)
    b_idx = jnp.arange(b)[:, None, None, None]
    h_idx = jnp.arange(h)[None, :, None, None]
    scale = HEAD_DIM ** -0.5

    def chunk(args):
        qi, si, vi, ci = args
        kg = kb[b_idx, h_idx, si]
        vg = vb[b_idx, h_idx, si].astype(jnp.float32)
        q_pos = ci * c + jnp.arange(c)
        start = (ci * c) // MOBA_BLOCK * MOBA_BLOCK
        k_own = lax.dynamic_slice_in_dim(k, start, MOBA_BLOCK, axis=2)
        v_own = lax.dynamic_slice_in_dim(v, start, MOBA_BLOCK, axis=2).astype(jnp.float32)
        s_sel = jnp.einsum('bhqd,bhqnkd->bhqnk', qi, kg).astype(jnp.float32) * scale
        s_sel = jnp.where(vi[None, None, :, :, None], s_sel, NEG_INF).reshape(b, h, c, n_sel * MOBA_BLOCK)
        s_own = jnp.einsum('bhqd,bhkd->bhqk', qi, k_own).astype(jnp.float32) * scale
        k_pos = start + jnp.arange(MOBA_BLOCK)
        s_own = jnp.where(k_pos[None, :] <= q_pos[:, None], s_own, NEG_INF)
        p = jax.nn.softmax(jnp.concatenate([s_sel, s_own], axis=-1), axis=-1)
        p_sel = p[..., :n_sel * MOBA_BLOCK].reshape(b, h, c, n_sel, MOBA_BLOCK)
        p_own = p[..., n_sel * MOBA_BLOCK:]
        return (jnp.einsum('bhqnk,bhqnkd->bhqd', p_sel, vg)
                + jnp.einsum('bhqk,bhkd->bhqd', p_own, v_own))

    o = lax.map(chunk, (qc, selc, validc, jnp.arange(nc)))
    o = o.transpose(1, 2, 0, 3, 4).reshape(b, h, sp, HEAD_DIM)[:, :, :s]
    return from_heads(o)


def dilated_branch(q, k, v, window, dil):
    b, h, s, d = q.shape
    span = dil * SW_BLOCK
    sp = -(-s // span) * span
    m_len = sp // dil
    nb = m_len // SW_BLOCK
    w_sub = window // dil

    def sub(t):
        t = jnp.pad(t, ((0, 0), (0, 0), (0, sp - s), (0, 0)))
        return t.reshape(b, h, m_len, dil, d).transpose(0, 1, 3, 2, 4).reshape(b, h, dil, nb, SW_BLOCK, d)

    def band(t):
        prev = jnp.pad(t, ((0, 0), (0, 0), (0, 0), (1, 0), (0, 0), (0, 0)))[:, :, :, :-1]
        return jnp.concatenate([prev, t], axis=4)

    qs = sub(q)
    kw = band(sub(k))
    vw = band(sub(v)).astype(jnp.float32)
    sc = jnp.einsum('bhrnqd,bhrnkd->bhrnqk', qs, kw).astype(jnp.float32) * (d ** -0.5)
    qi = jnp.arange(SW_BLOCK)[:, None]
    kj = jnp.arange(2 * SW_BLOCK)[None, :] - SW_BLOCK
    dist = qi - kj
    blk = jnp.arange(nb)[:, None, None]
    mask = (dist >= 0) & (dist <= w_sub) & (blk * SW_BLOCK + kj >= 0)
    sc = jnp.where(mask, sc, NEG_INF)
    mx = jnp.max(sc, axis=-1, keepdims=True)
    e = jnp.exp(sc - mx)
    den = jnp.sum(e, axis=-1)
    o = jnp.einsum('bhrnqk,bhrnkd->bhrnqd', e, vw) / den[..., None]
    lse = mx[..., 0] + jnp.log(den)
    o = o.reshape(b, h, dil, m_len, d).transpose(0, 1, 3, 2, 4).reshape(b, h, sp, d)[:, :, :s]
    lse = lse.reshape(b, h, dil, m_len).transpose(0, 1, 3, 2).reshape(b, h, sp)[:, :, :s]
    return o, lse


def dilated_attention(q, k, v, positions):
    h = N_HEADS_C
    q = partial_rope(to_heads(q, h), positions)
    k = partial_rope(to_heads(k, h), positions)
    v = to_heads(v, h)
    outs, lses = [], []
    for window, dil in DILATED_PATTERNS:
        o, lse = dilated_branch(q, k, v, window, dil)
        outs.append(o)
        lses.append(lse)
    wts = jax.nn.softmax(jnp.stack(lses, axis=0), axis=0)
    o = jnp.sum(wts[..., None] * jnp.stack(outs, axis=0), axis=0)
    return from_heads(o)


def conv_gated_mlp(h, w_up, conv_w, conv_b, w_down):
    u = h @ w_up
    u = lax.conv_general_dilated(u, conv_w[:, None, :].astype(u.dtype), window_strides=(1,),
                                 padding=[(CONV_WIDTH - 1, 0)],
                                 dimension_numbers=('NWC', 'WIO', 'NWC'),
                                 feature_group_count=u.shape[-1]) + conv_b
    gate, val = jnp.split(u, 2, axis=-1)
    return (jax.nn.silu(gate) * val) @ w_down


def setup_inputs(seed: int = 0) -> dict:
    key = jax.random.key(seed)
    ks = jax.random.split(key, 16)
    nrm = jax.random.normal
    return {
        'x': nrm(ks[0], (BATCH, SEQ, D_MODEL), jnp.float32),
        'positions': jnp.broadcast_to(jnp.arange(SEQ, dtype=jnp.int32)[None, :], (BATCH, SEQ)),
        'g_mix': 1.0 + 0.02 * nrm(ks[1], (DEPTH, D_MODEL), jnp.float32),
        'w_in': nrm(ks[2], (DEPTH, D_MODEL, 3 * D_MODEL), jnp.float32) * D_MODEL ** -0.5,
        'w_out': nrm(ks[3], (DEPTH, D_MODEL, D_MODEL), jnp.float32) * D_MODEL ** -0.5,
        'lambda_q1': 0.1 * nrm(ks[4], (DEPTH, DIFF_DIM), jnp.float32),
        'lambda_k1': 0.1 * nrm(ks[5], (DEPTH, DIFF_DIM), jnp.float32),
        'lambda_q2': 0.1 * nrm(ks[6], (DEPTH, DIFF_DIM), jnp.float32),
        'lambda_k2': 0.1 * nrm(ks[7], (DEPTH, DIFF_DIM), jnp.float32),
        'g_diff': 1.0 + 0.02 * nrm(ks[8], (DEPTH, HEAD_DIM), jnp.float32),
        'g_ffn': 1.0 + 0.02 * nrm(ks[9], (DEPTH, D_MODEL), jnp.float32),
        'w_up': nrm(ks[10], (DEPTH, D_MODEL, 2 * D_FF), jnp.float32) * D_MODEL ** -0.5,
        'conv_w': nrm(ks[11], (DEPTH, CONV_WIDTH, 2 * D_FF), jnp.float32) * CONV_WIDTH ** -0.5,
        'conv_b': 0.01 * nrm(ks[12], (DEPTH, 2 * D_FF), jnp.float32),
        'w_down': nrm(ks[13], (DEPTH, D_FF, D_MODEL), jnp.float32) * D_FF ** -0.5,
        'g_final': 1.0 + 0.02 * nrm(ks[14], (D_MODEL,), jnp.float32),
    }


def reference(x, positions, g_mix, w_in, w_out, lambda_q1, lambda_k1, lambda_q2, lambda_k2,
              g_diff, g_ffn, w_up, conv_w, conv_b, w_down, g_final):
    widths = [WIDTH_A] * 3 + [WIDTH_B] * 3 + [WIDTH_C] * 3
    cuts = []
    acc = 0
    for wd in widths[:-1]:
        acc += wd
        cuts.append(acc)
    for layer in range(DEPTH):
        h = rms_norm(x, g_mix[layer])
        qkv = h @ w_in[layer]
        qa, ka, va, qb, kb, vb, qc, kc, vc = jnp.split(qkv, cuts, axis=-1)
        lam_init = 0.8 - 0.6 * math.exp(-0.3 * layer)
        lam = (jnp.exp(jnp.sum(lambda_q1[layer].astype(jnp.float32) * lambda_k1[layer].astype(jnp.float32)))
               - jnp.exp(jnp.sum(lambda_q2[layer].astype(jnp.float32) * lambda_k2[layer].astype(jnp.float32)))
               + lam_init)
        o_a = differential_attention(qa, ka, va, positions, lam, g_diff[layer], lam_init).astype(x.dtype)
        o_b = moba_attention(qb, kb, vb, positions).astype(x.dtype)
        o_c = dilated_attention(qc, kc, vc, positions).astype(x.dtype)
        mix = jnp.concatenate([o_a, o_b, o_c], axis=-1)
        x = x + mix @ w_out[layer]
        h2 = rms_norm(x, g_ffn[layer])
        x = x + conv_gated_mlp(h2, w_up[layer], conv_w[layer], conv_b[layer], w_down[layer]).astype(x.dtype)
    return rms_norm(x, g_final)
```

```python
import functools
import math

import numpy as np
import jax
import jax.numpy as jnp
from jax import lax
from jax.experimental import pallas as pl
from jax.experimental.pallas import tpu as pltpu

F32 = jnp.float32
BF16 = jnp.bfloat16

D_MODEL = 1024
HEAD_DIM = 64
DIFF_DIM = 32
WIDTH_A = 256
WIDTH_B = 256
WIDTH_C = 512
ROPE_THETA = 500000.0
MOBA_BLOCK = 256
MOBA_TOPK = 3
DILATED_PATTERNS = ((128, 1), (512, 4), (2048, 16))
D_FF = 2816
EPS = 1e-6
NEG_INF = -1e30

LANES = 128
ATT_BLOCK = 256
ROW_TILE = 512
FF_TILE = 256
HALO = 16
VMEM_LIMIT = 48 * 1024 * 1024

_SEGMENTS = (
    (0, 256, "qa"), (256, 256, "ka"), (512, 256, "v"),
    (768, 256, "q"), (1024, 256, "k"), (1280, 256, "v"),
    (1536, 512, "q"), (2048, 512, "k"), (2560, 512, "v"),
)


def _cparams(*sem):
    return pltpu.CompilerParams(dimension_semantics=sem, vmem_limit_bytes=VMEM_LIMIT)


def _rope_table_kernel(pos_ref, tab_ref):
    pos = pos_ref[...].astype(F32)
    lane = lax.broadcasted_iota(jnp.int32, (1, 2 * LANES), 1)
    is_a = lane < LANES
    d = jnp.where(is_a, lane & (DIFF_DIM - 1), lane & (HEAD_DIM - 1))
    half = jnp.where(is_a, DIFF_DIM // 8, HEAD_DIM // 8)
    rot = 2 * half
    j = jnp.where(d < half, d, d - half)
    expo = -(j.astype(F32)) * 2.0 / rot.astype(F32)
    inv = jnp.where(d < rot, jnp.power(jnp.float32(ROPE_THETA), expo), 0.0)
    sign = jnp.where(d < half, -1.0, 1.0).astype(F32)
    ang = pos * inv
    tab_ref[:, : 2 * LANES] = jnp.cos(ang)
    tab_ref[:, 2 * LANES:] = jnp.sin(ang) * sign


def _rope_tables(positions):
    m = positions.size
    tr = ROW_TILE
    return pl.pallas_call(
        _rope_table_kernel,
        grid=(m // tr,),
        in_specs=[pl.BlockSpec((tr, 1), lambda i: (i, 0))],
        out_specs=pl.BlockSpec((tr, 4 * LANES), lambda i: (i, 0)),
        out_shape=jax.ShapeDtypeStruct((m, 4 * LANES), F32),
        compiler_params=_cparams("parallel"),
        name="rope_tables",
    )(positions.reshape(m, 1))


def _proj_in_kernel(x_ref, g_ref, w_ref, tab_ref, o_ref):
    x = x_ref[...]
    ms = jnp.mean(x * x, axis=-1, keepdims=True)
    h = (x * lax.rsqrt(ms + EPS) * g_ref[...]).astype(BF16)
    lane = lax.broadcasted_iota(jnp.int32, (1, LANES), 1)
    for start, width, kind in _SEGMENTS:
        acc = jnp.dot(h, w_ref[:, start:start + width], preferred_element_type=F32)
        if kind == "v":
            o_ref[:, start:start + width] = acc.astype(BF16)
            continue
        narrow = kind.endswith("a")
        dim = DIFF_DIM if narrow else HEAD_DIM
        half = dim // 8
        off = 0 if narrow else LANES
        cos_t = tab_ref[:, off:off + LANES]
        sin_t = tab_ref[:, 2 * LANES + off:3 * LANES + off]
        upper = (lane & (dim - 1)) >= half
        scale = dim ** -0.5 if kind.startswith("q") else None
        for c in range(width // LANES):
            xs = acc[:, c * LANES:(c + 1) * LANES]
            partner = jnp.where(upper, pltpu.roll(xs, half, 1), pltpu.roll(xs, LANES - half, 1))
            y = xs * cos_t + partner * sin_t
            if scale is not None:
                y = y * scale
            o_ref[:, start + c * LANES:start + (c + 1) * LANES] = y.astype(BF16)


def _proj_in(x2d, g, w, tab):
    m = x2d.shape[0]
    tm = ROW_TILE
    return pl.pallas_call(
        _proj_in_kernel,
        grid=(m // tm,),
        in_specs=[
            pl.BlockSpec((tm, D_MODEL), lambda i: (i, 0)),
            pl.BlockSpec((1, D_MODEL), lambda i: (0, 0)),
            pl.BlockSpec((D_MODEL, 3 * D_MODEL), lambda i: (0, 0)),
            pl.BlockSpec((tm, 4 * LANES), lambda i: (i, 0)),
        ],
        out_specs=pl.BlockSpec((tm, 3 * D_MODEL), lambda i: (i, 0)),
        out_shape=jax.ShapeDtypeStruct((m, 3 * D_MODEL), BF16),
        compiler_params=_cparams("parallel"),
        name="proj_in",
    )(x2d, g, w, tab)


def _nt_dot(a, b):
    return lax.dot_general(a, b, (((1,), (1,)), ((), ())), preferred_element_type=F32)


def _rep(col):
    return jnp.broadcast_to(col, (col.shape[0], LANES))


def _softmax_first(s, v, m_ref, l_ref, acc_ref, i):
    m = jnp.max(s, axis=1, keepdims=True)
    p = jnp.exp(s - m)
    m_ref[i] = _rep(m)
    l_ref[i] = _rep(jnp.sum(p, axis=1, keepdims=True))
    acc_ref[i] = jnp.dot(p.astype(BF16), v, preferred_element_type=F32)


def _softmax_next(s, v, m_ref, l_ref, acc_ref, i):
    m_prev = m_ref[i][:, :1]
    m_new = jnp.maximum(m_prev, jnp.max(s, axis=1, keepdims=True))
    alpha = jnp.exp(m_prev - m_new)
    p = jnp.exp(s - m_new)
    m_ref[i] = _rep(m_new)
    l_ref[i] = _rep(alpha * l_ref[i][:, :1] + jnp.sum(p, axis=1, keepdims=True))
    acc_ref[i] = alpha * acc_ref[i] + jnp.dot(p.astype(BF16), v, preferred_element_type=F32)


def _causal_mask(t):
    r = lax.broadcasted_iota(jnp.int32, (t, t), 0)
    c = lax.broadcasted_iota(jnp.int32, (t, t), 1)
    return c <= r


def _kv_block(ref, j):
    return ref[pl.ds(pl.multiple_of(j * ATT_BLOCK, ATT_BLOCK), ATT_BLOCK), :]


def _diff_attn_kernel(q_ref, k_ref, v_ref, lq1_ref, lk1_ref, lq2_ref, lk2_ref, g_ref, o_ref,
                      qm_ref, m_ref, l_ref, acc_ref, *, lam_init):
    qi = pl.program_id(2)
    t = ATT_BLOCK
    lane = lax.broadcasted_iota(jnp.int32, (1, LANES), 1)
    q = q_ref[...]
    for i in range(4):
        in_map = (lane >= i * DIFF_DIM) & (lane < (i + 1) * DIFF_DIM)
        qm_ref[i] = jnp.where(in_map, q, jnp.zeros_like(q))

    k_d = _kv_block(k_ref, qi)
    v_d = _kv_block(v_ref, qi)
    mask = _causal_mask(t)
    for i in range(4):
        s = jnp.where(mask, _nt_dot(qm_ref[i], k_d), NEG_INF)
        _softmax_first(s, v_d, m_ref, l_ref, acc_ref, i)

    def body(j, carry):
        k_j = _kv_block(k_ref, j)
        v_j = _kv_block(v_ref, j)
        for i in range(4):
            _softmax_next(_nt_dot(qm_ref[i], k_j), v_j, m_ref, l_ref, acc_ref, i)
        return carry

    lax.fori_loop(0, qi, body, 0)

    lam = (jnp.exp(jnp.sum(lq1_ref[...] * lk1_ref[...], axis=1, keepdims=True))
           - jnp.exp(jnp.sum(lq2_ref[...] * lk2_ref[...], axis=1, keepdims=True)) + lam_init)
    o0 = acc_ref[0] / l_ref[0] - lam * (acc_ref[1] / l_ref[1])
    o1 = acc_ref[2] / l_ref[2] - lam * (acc_ref[3] / l_ref[3])
    first = lane < HEAD_DIM
    o = jnp.where(first, o0, o1)
    sq = o * o
    ms0 = jnp.sum(jnp.where(first, sq, 0.0), axis=1, keepdims=True) / HEAD_DIM
    ms1 = jnp.sum(jnp.where(first, 0.0, sq), axis=1, keepdims=True) / HEAD_DIM
    ms = jnp.where(first, ms0, ms1)
    y = o * lax.rsqrt(ms + EPS) * g_ref[...] * (1.0 - lam_init)
    o_ref[...] = y.astype(BF16)


def _diff_attn(qkv, lq1, lk1, lq2, lk2, g2, batch, seq, lam_init):
    t = ATT_BLOCK
    nq = seq // t
    vec = pl.BlockSpec((1, DIFF_DIM), lambda b, hp, qi: (0, 0))
    return pl.pallas_call(
        functools.partial(_diff_attn_kernel, lam_init=lam_init),
        grid=(batch, WIDTH_A // LANES, nq),
        in_specs=[
            pl.BlockSpec((t, LANES), lambda b, hp, qi: (b * nq + qi, hp)),
            pl.BlockSpec((seq, LANES), lambda b, hp, qi: (b, 2 + hp)),
            pl.BlockSpec((seq, LANES), lambda b, hp, qi: (b, 4 + hp)),
            vec, vec, vec, vec,
            pl.BlockSpec((1, LANES), lambda b, hp, qi: (0, 0)),
        ],
        out_specs=pl.BlockSpec((t, LANES), lambda b, hp, qi: (b * nq + qi, hp)),
        out_shape=jax.ShapeDtypeStruct((batch * seq, WIDTH_A), BF16),
        scratch_shapes=[
            pltpu.VMEM((4, t, LANES), BF16),
            pltpu.VMEM((4, t, LANES), F32),
            pltpu.VMEM((4, t, LANES), F32),
            pltpu.VMEM((4, t, LANES), F32),
        ],
        compiler_params=_cparams("parallel", "parallel", "arbitrary"),
        name="diff_attn",
    )(qkv, qkv, qkv, lq1, lk1, lq2, lk2, g2)


def _moba_attn_kernel(q_ref, k_ref, v_ref, o_ref, kmean_ref, qc_ref, m_ref, l_ref, acc_ref, *, nb):
    qi = pl.program_id(2)
    t = ATT_BLOCK
    lane1 = lax.broadcasted_iota(jnp.int32, (1, LANES), 1)

    @pl.when(qi == 0)
    def _():
        kmean_ref[...] = jnp.zeros_like(kmean_ref)
        for j in range(nb):
            kb = k_ref[j * t:(j + 1) * t, :].astype(F32)
            kmean_ref[j:j + 1, :] = jnp.mean(kb, axis=0, keepdims=True)

    q = q_ref[...]
    kmean = kmean_ref[...].astype(BF16)
    lane = lax.broadcasted_iota(jnp.int32, (t, LANES), 1)
    for hh in range(2):
        in_head = (lane1 >= hh * HEAD_DIM) & (lane1 < (hh + 1) * HEAD_DIM)
        qm = jnp.where(in_head, q, jnp.zeros_like(q))
        gate = _nt_dot(qm, kmean)
        cnt = jnp.zeros((t, LANES), jnp.int32)
        for i in range(nb):
            gi = gate[:, i:i + 1]
            beats = (gi > gate) | ((gi == gate) & (lane > i))
            live = jnp.where(i < qi, 1, 0)
            cnt = cnt + jnp.where(beats, live, 0)
        sel = (cnt < MOBA_TOPK) & (lane < qi)
        bias = jnp.where(sel, 0.0, NEG_INF).astype(BF16)
        qc_ref[hh, :, :LANES] = qm
        qc_ref[hh, :, LANES:] = bias

    k_d = _kv_block(k_ref, qi)
    v_d = _kv_block(v_ref, qi)
    mask = _causal_mask(t)
    for hh in range(2):
        s = jnp.where(mask, _nt_dot(qc_ref[hh, :, :LANES], k_d), NEG_INF)
        _softmax_first(s, v_d, m_ref, l_ref, acc_ref, hh)

    def body(j, carry):
        k_j = _kv_block(k_ref, j)
        v_j = _kv_block(v_ref, j)
        onehot = jnp.where(lane == j, 1.0, 0.0).astype(BF16)
        kc = jnp.concatenate([k_j, onehot], axis=1)
        for hh in range(2):
            _softmax_next(_nt_dot(qc_ref[hh], kc), v_j, m_ref, l_ref, acc_ref, hh)
        return carry

    lax.fori_loop(0, qi, body, 0)

    o0 = acc_ref[0] / l_ref[0]
    o1 = acc_ref[1] / l_ref[1]
    o_ref[...] = jnp.where(lane1 < HEAD_DIM, o0, o1).astype(BF16)


def _moba_attn(qkv, batch, seq):
    t = ATT_BLOCK
    nq = seq // t
    col0 = (WIDTH_A * 3) // LANES
    nhp = WIDTH_B // LANES
    return pl.pallas_call(
        functools.partial(_moba_attn_kernel, nb=seq // MOBA_BLOCK),
        grid=(batch, nhp, nq),
        in_specs=[
            pl.BlockSpec((t, LANES), lambda b, hp, qi: (b * nq + qi, col0 + hp)),
            pl.BlockSpec((seq, LANES), lambda b, hp, qi: (b, col0 + nhp + hp)),
            pl.BlockSpec((seq, LANES), lambda b, hp, qi: (b, col0 + 2 * nhp + hp)),
        ],
        out_specs=pl.BlockSpec((t, LANES), lambda b, hp, qi: (b * nq + qi, hp)),
        out_shape=jax.ShapeDtypeStruct((batch * seq, WIDTH_B), BF16),
        scratch_shapes=[
            pltpu.VMEM((LANES, LANES), F32),
            pltpu.VMEM((2, t, 2 * LANES), BF16),
            pltpu.VMEM((2, t, LANES), F32),
            pltpu.VMEM((2, t, LANES), F32),
            pltpu.VMEM((2, t, LANES), F32),
        ],
        compiler_params=_cparams("parallel", "parallel", "arbitrary"),
        name="moba_attn",
    )(qkv, qkv, qkv)


def _dilated_bias_table():
    t = ATT_BLOCK
    reach = max(w for w, _ in DILATED_PATTERNS)
    nd = reach // t + 1
    d = (np.arange(nd)[:, None, None] * t + np.arange(t)[None, :, None] - np.arange(t)[None, None, :])
    mult = np.zeros(d.shape, np.int32)
    for window, dil in DILATED_PATTERNS:
        mult += ((d >= 0) & (d <= window) & (d % dil == 0)).astype(np.int32)
    return np.where(mult > 0, np.log(np.maximum(mult, 1).astype(np.float64)), NEG_INF).astype(np.float32)


def _dilated_attn_kernel(q_ref, k_ref, v_ref, bias_ref, o_ref, qm_ref, m_ref, l_ref, acc_ref, *, nd):
    qi = pl.program_id(2)
    lane1 = lax.broadcasted_iota(jnp.int32, (1, LANES), 1)
    q = q_ref[...]
    for hh in range(2):
        in_head = (lane1 >= hh * HEAD_DIM) & (lane1 < (hh + 1) * HEAD_DIM)
        qm_ref[hh] = jnp.where(in_head, q, jnp.zeros_like(q))

    k_d = _kv_block(k_ref, qi)
    v_d = _kv_block(v_ref, qi)
    for hh in range(2):
        _softmax_first(_nt_dot(qm_ref[hh], k_d) + bias_ref[0], v_d, m_ref, l_ref, acc_ref, hh)

    def body(j, carry):
        k_j = _kv_block(k_ref, j)
        v_j = _kv_block(v_ref, j)
        bias = bias_ref[qi - j]
        for hh in range(2):
            _softmax_next(_nt_dot(qm_ref[hh], k_j) + bias, v_j, m_ref, l_ref, acc_ref, hh)
        return carry

    lax.fori_loop(jnp.maximum(qi - (nd - 1), 0), qi, body, 0)

    o0 = acc_ref[0] / l_ref[0]
    o1 = acc_ref[1] / l_ref[1]
    o_ref[...] = jnp.where(lane1 < HEAD_DIM, o0, o1).astype(BF16)


def _dilated_attn(qkv, bias, batch, seq):
    t = ATT_BLOCK
    nq = seq // t
    nd = bias.shape[0]
    col0 = (WIDTH_A * 3 + WIDTH_B * 3) // LANES
    nhp = WIDTH_C // LANES
    return pl.pallas_call(
        functools.partial(_dilated_attn_kernel, nd=nd),
        grid=(batch, nhp, nq),
        in_specs=[
            pl.BlockSpec((t, LANES), lambda b, hp, qi: (b * nq + qi, col0 + hp)),
            pl.BlockSpec((seq, LANES), lambda b, hp, qi: (b, col0 + nhp + hp)),
            pl.BlockSpec((seq, LANES), lambda b, hp, qi: (b, col0 + 2 * nhp + hp)),
            pl.BlockSpec((nd, t, t), lambda b, hp, qi: (0, 0, 0)),
        ],
        out_specs=pl.BlockSpec((t, LANES), lambda b, hp, qi: (b * nq + qi, hp)),
        out_shape=jax.ShapeDtypeStruct((batch * seq, WIDTH_C), BF16),
        scratch_shapes=[
            pltpu.VMEM((2, t, LANES), BF16),
            pltpu.VMEM((2, t, LANES), F32),
            pltpu.VMEM((2, t, LANES), F32),
            pltpu.VMEM((2, t, LANES), F32),
        ],
        compiler_params=_cparams("parallel", "parallel", "arbitrary"),
        name="dilated_attn",
    )(qkv, qkv, qkv, bias)


def _proj_out_kernel(oa_ref, ob_ref, oc_ref, w_ref, x_ref, g_ref, xo_ref, h_ref):
    acc = jnp.dot(oa_ref[...], w_ref[:WIDTH_A, :], preferred_element_type=F32)
    acc = acc + jnp.dot(ob_ref[...], w_ref[WIDTH_A:WIDTH_A + WIDTH_B, :], preferred_element_type=F32)
    acc = acc + jnp.dot(oc_ref[...], w_ref[WIDTH_A + WIDTH_B:, :], preferred_element_type=F32)
    xn = x_ref[...] + acc
    xo_ref[...] = xn
    ms = jnp.mean(xn * xn, axis=-1, keepdims=True)
    h_ref[...] = (xn * lax.rsqrt(ms + EPS) * g_ref[...]).astype(BF16)


def _proj_out(oa, ob, oc, w, x2d, g):
    m = x2d.shape[0]
    tm = ROW_TILE
    row = lambda i: (i, 0)
    fixed = lambda i: (0, 0)
    return pl.pallas_call(
        _proj_out_kernel,
        grid=(m // tm,),
        in_specs=[
            pl.BlockSpec((tm, WIDTH_A), row),
            pl.BlockSpec((tm, WIDTH_B), row),
            pl.BlockSpec((tm, WIDTH_C), row),
            pl.BlockSpec((D_MODEL, D_MODEL), fixed),
            pl.BlockSpec((tm, D_MODEL), row),
            pl.BlockSpec((1, D_MODEL), fixed),
        ],
        out_specs=[pl.BlockSpec((tm, D_MODEL), row), pl.BlockSpec((tm, D_MODEL), row)],
        out_shape=[jax.ShapeDtypeStruct((m, D_MODEL), F32), jax.ShapeDtypeStruct((m, D_MODEL), BF16)],
        compiler_params=_cparams("parallel"),
        name="proj_out",
    )(oa, ob, oc, w, x2d, g)


def _mlp_kernel(h_ref, halo_ref, x_ref, wg_ref, wv_ref, cwg_ref, cwv_ref, cbg_ref, cbv_ref, wd_ref,
                gf_ref, o_ref, acc_ref, u_ref, *, tiles_per_seq, final_norm):
    i = pl.program_id(0)
    j = pl.program_id(1)
    tm = h_ref.shape[0]
    seq_start = (i % tiles_per_seq) == 0
    hcat = jnp.concatenate([halo_ref[...], h_ref[...]], axis=0)

    def conv_branch(w_ref, cw_ref, cb_ref, slot):
        u = jnp.dot(hcat, w_ref[...], preferred_element_type=F32)
        u_ref[slot] = u
        u_ref[slot, :HALO, :] = jnp.where(seq_start, 0.0, u[:HALO])
        cw = cw_ref[...]
        return (cw[2:3] * u_ref[slot, HALO:HALO + tm, :]
                + cw[1:2] * u_ref[slot, HALO - 1:HALO - 1 + tm, :]
                + cw[0:1] * u_ref[slot, HALO - 2:HALO - 2 + tm, :]
                + cb_ref[...])

    gate = conv_branch(wg_ref, cwg_ref, cbg_ref, 0)
    val = conv_branch(wv_ref, cwv_ref, cbv_ref, 1)
    act = (gate * jax.nn.sigmoid(gate) * val).astype(BF16)
    part = jnp.dot(act, wd_ref[...], preferred_element_type=F32)

    @pl.when(j == 0)
    def _():
        acc_ref[...] = part

    @pl.when(j > 0)
    def _():
        acc_ref[...] += part

    @pl.when(j == pl.num_programs(1) - 1)
    def _():
        y = x_ref[...] + acc_ref[...]
        if final_norm:
            ms = jnp.mean(y * y, axis=-1, keepdims=True)
            y = y * lax.rsqrt(ms + EPS) * gf_ref[...]
        o_ref[...] = y


def _mlp(h, x2d, w_up, conv_w, conv_b, w_down, g_final, seq, final_norm):
    m = x2d.shape[0]
    tm, tf = ROW_TILE, FF_TILE
    nf = D_FF // tf
    hb = tm // HALO
    row = lambda i, j: (i, 0)
    return pl.pallas_call(
        functools.partial(_mlp_kernel, tiles_per_seq=seq // tm, final_norm=final_norm),
        grid=(m // tm, nf),
        in_specs=[
            pl.BlockSpec((tm, D_MODEL), row),
            pl.BlockSpec((HALO, D_MODEL), lambda i, j: (jnp.maximum(i * hb - 1, 0), 0)),
            pl.BlockSpec((tm, D_MODEL), row),
            pl.BlockSpec((D_MODEL, tf), lambda i, j: (0, j)),
            pl.BlockSpec((D_MODEL, tf), lambda i, j: (0, nf + j)),
            pl.BlockSpec((3, tf), lambda i, j: (0, j)),
            pl.BlockSpec((3, tf), lambda i, j: (0, nf + j)),
            pl.BlockSpec((1, tf), lambda i, j: (0, j)),
            pl.BlockSpec((1, tf), lambda i, j: (0, nf + j)),
            pl.BlockSpec((tf, D_MODEL), lambda i, j: (j, 0)),
            pl.BlockSpec((1, D_MODEL), lambda i, j: (0, 0)),
        ],
        out_specs=pl.BlockSpec((tm, D_MODEL), row),
        out_shape=jax.ShapeDtypeStruct((m, D_MODEL), F32),
        scratch_shapes=[
            pltpu.VMEM((tm, D_MODEL), F32),
            pltpu.VMEM((2, HALO + tm, tf), F32),
        ],
        compiler_params=_cparams("parallel", "arbitrary"),
        name="conv_mlp",
    )(h, h, x2d, w_up, w_up, conv_w, conv_w, conv_b, conv_b, w_down, g_final)


def kernel(x, positions, g_mix, w_in, w_out, lambda_q1, lambda_k1, lambda_q2, lambda_k2, g_diff, g_ffn,
           w_up, conv_w, conv_b, w_down, g_final):
    batch, seq, d_model = x.shape
    depth = g_mix.shape[0]
    assert d_model == D_MODEL and seq % max(ROW_TILE, MOBA_BLOCK, ATT_BLOCK) == 0
    m = batch * seq
    tab = _rope_tables(positions)
    bias = jnp.asarray(_dilated_bias_table())
    xf = x.reshape(m, d_model)
    for layer in range(depth):
        lam_init = 0.8 - 0.6 * math.exp(-0.3 * layer)
        qkv = _proj_in(xf, g_mix[layer][None, :], w_in[layer].astype(BF16), tab)
        o_a = _diff_attn(qkv, lambda_q1[layer][None, :], lambda_k1[layer][None, :],
                         lambda_q2[layer][None, :], lambda_k2[layer][None, :],
                         jnp.tile(g_diff[layer], 2)[None, :], batch, seq, lam_init)
        o_b = _moba_attn(qkv, batch, seq)
        o_c = _dilated_attn(qkv, bias, batch, seq)
        xf, h2 = _proj_out(o_a, o_b, o_c, w_out[layer].astype(BF16), xf, g_ffn[layer][None, :])
        xf = _mlp(h2, xf, w_up[layer].astype(BF16), conv_w[layer], conv_b[layer][None, :],
                  w_down[layer].astype(BF16), g_final[None, :], seq, layer == depth - 1)
    return xf.reshape(batch, seq, d_model)
```

```python
import functools
import math

import numpy as np
import jax
import jax.numpy as jnp
from jax import lax
from jax.experimental import pallas as pl
from jax.experimental.pallas import tpu as pltpu

F32 = jnp.float32
BF16 = jnp.bfloat16

D_MODEL = 1024
HEAD_DIM = 64
DIFF_DIM = 32
WIDTH_A = 256
WIDTH_B = 256
WIDTH_C = 512
ROPE_THETA = 500000.0
MOBA_BLOCK = 256
MOBA_TOPK = 3
DILATED_PATTERNS = ((128, 1), (512, 4), (2048, 16))
D_FF = 2816
EPS = 1e-6
NEG_INF = -1e30
LOG2E = math.log2(math.e)

LANES = 128
ATT_BLOCK = 256
ROW_TILE = 512
FF_TILE = 256
HALO = 16
VMEM_LIMIT = 48 * 1024 * 1024

_QK_SEGMENTS = (
    (0, 256, "qa"), (256, 256, "ka"),
    (512, 256, "q"), (768, 256, "k"),
    (1024, 512, "q"), (1536, 512, "k"),
)
_QK_WIDTH = 2048
_V_WIDTH = 1024
_QA_BLK, _KA_BLK, _QB_BLK, _KB_BLK, _QC_BLK, _KC_BLK = 0, 2, 4, 6, 8, 12
_VA_BLK, _VB_BLK, _VC_BLK = 0, 2, 4


def _cparams(*sem):
    return pltpu.CompilerParams(dimension_semantics=sem, vmem_limit_bytes=VMEM_LIMIT)


def _rope_table_kernel(pos_ref, tab_ref):
    pos = pos_ref[...].astype(F32)
    lane = lax.broadcasted_iota(jnp.int32, (1, 2 * LANES), 1)
    is_a = lane < LANES
    d = jnp.where(is_a, lane & (DIFF_DIM - 1), lane & (HEAD_DIM - 1))
    half = jnp.where(is_a, DIFF_DIM // 8, HEAD_DIM // 8)
    rot = 2 * half
    j = jnp.where(d < half, d, d - half)
    expo = -(j.astype(F32)) * 2.0 / rot.astype(F32)
    inv = jnp.where(d < rot, jnp.power(jnp.float32(ROPE_THETA), expo), 0.0)
    sign = jnp.where(d < half, -1.0, 1.0).astype(F32)
    ang = pos * inv
    tab_ref[:, : 2 * LANES] = jnp.cos(ang)
    tab_ref[:, 2 * LANES:] = jnp.sin(ang) * sign


def _rope_tables(positions):
    m = positions.size
    tr = ROW_TILE
    return pl.pallas_call(
        _rope_table_kernel,
        grid=(m // tr,),
        in_specs=[pl.BlockSpec((tr, 1), lambda i: (i, 0))],
        out_specs=pl.BlockSpec((tr, 4 * LANES), lambda i: (i, 0)),
        out_shape=jax.ShapeDtypeStruct((m, 4 * LANES), F32),
        compiler_params=_cparams("parallel"),
        name="rope_tables",
    )(positions.reshape(m, 1))


def _nt_dot(a, b):
    return lax.dot_general(a, b, (((1,), (1,)), ((), ())), preferred_element_type=F32)


def _proj_in_kernel(x_ref, g_ref, wqk_ref, wvt_ref, tab_ref, qk_ref, vt_ref):
    x = x_ref[...]
    ms = jnp.mean(x * x, axis=-1, keepdims=True)
    h = (x * lax.rsqrt(ms + EPS) * g_ref[...]).astype(BF16)
    lane = lax.broadcasted_iota(jnp.int32, (1, LANES), 1)
    for start, width, kind in _QK_SEGMENTS:
        acc = jnp.dot(h, wqk_ref[:, start:start + width], preferred_element_type=F32)
        narrow = kind.endswith("a")
        dim = DIFF_DIM if narrow else HEAD_DIM
        half = dim // 8
        off = 0 if narrow else LANES
        cos_t = tab_ref[:, off:off + LANES]
        sin_t = tab_ref[:, 2 * LANES + off:3 * LANES + off]
        upper = (lane & (dim - 1)) >= half
        scale = dim ** -0.5 * LOG2E if kind.startswith("q") else None
        for c in range(width // LANES):
            xs = acc[:, c * LANES:(c + 1) * LANES]
            partner = jnp.where(upper, pltpu.roll(xs, half, 1), pltpu.roll(xs, LANES - half, 1))
            y = xs * cos_t + partner * sin_t
            if scale is not None:
                y = y * scale
            qk_ref[:, start + c * LANES:start + (c + 1) * LANES] = y.astype(BF16)
    t = ATT_BLOCK
    for r in range(_V_WIDTH // t):
        vt = _nt_dot(wvt_ref[r * t:(r + 1) * t, :], h).astype(BF16)
        for c in range(vt_ref.shape[0]):
            vt_ref[c, r * t:(r + 1) * t, :] = vt[:, c * t:(c + 1) * t]


def _proj_in(x2d, g, wqk, wvt, tab):
    m = x2d.shape[0]
    tm, t = ROW_TILE, ATT_BLOCK
    return pl.pallas_call(
        _proj_in_kernel,
        grid=(m // tm,),
        in_specs=[
            pl.BlockSpec((tm, D_MODEL), lambda i: (i, 0)),
            pl.BlockSpec((1, D_MODEL), lambda i: (0, 0)),
            pl.BlockSpec((D_MODEL, _QK_WIDTH), lambda i: (0, 0)),
            pl.BlockSpec((_V_WIDTH, D_MODEL), lambda i: (0, 0)),
            pl.BlockSpec((tm, 4 * LANES), lambda i: (i, 0)),
        ],
        out_specs=[
            pl.BlockSpec((tm, _QK_WIDTH), lambda i: (i, 0)),
            pl.BlockSpec((tm // t, _V_WIDTH, t), lambda i: (i, 0, 0)),
        ],
        out_shape=[
            jax.ShapeDtypeStruct((m, _QK_WIDTH), BF16),
            jax.ShapeDtypeStruct((m // t, _V_WIDTH, t), BF16),
        ],
        compiler_params=_cparams("parallel"),
        name="proj_in",
    )(x2d, g, wqk, wvt, tab)


def _softmax_first(s, vt, m_ref, l_ref, acc_ref, i):
    m = jnp.max(s, axis=0, keepdims=True)
    p = jnp.exp2(s - m)
    m_ref[i] = m
    l_ref[i] = jnp.sum(p, axis=0, keepdims=True)
    acc_ref[i] = jnp.dot(vt, p.astype(BF16), preferred_element_type=F32)


def _softmax_next(s, vt, m_ref, l_ref, acc_ref, i, col_bias=None):
    m_prev = m_ref[i]
    m_blk = jnp.max(s, axis=0, keepdims=True)
    if col_bias is not None:
        m_blk = m_blk + col_bias
    m_new = jnp.maximum(m_prev, m_blk)
    alpha = jnp.exp2(m_prev - m_new)
    shift = m_new if col_bias is None else m_new - col_bias
    p = jnp.exp2(s - shift)
    m_ref[i] = m_new
    l_ref[i] = alpha * l_ref[i] + jnp.sum(p, axis=0, keepdims=True)
    acc_ref[i] = alpha * acc_ref[i] + jnp.dot(vt, p.astype(BF16), preferred_element_type=F32)


def _causal_mask_t(t):
    key = lax.broadcasted_iota(jnp.int32, (t, t), 0)
    qry = lax.broadcasted_iota(jnp.int32, (t, t), 1)
    return key <= qry


def _k_block(ref, j):
    return ref[pl.ds(pl.multiple_of(j * ATT_BLOCK, ATT_BLOCK), ATT_BLOCK), :]


def _masked_stack(k_blk, lane, width, n):
    zero = jnp.zeros_like(k_blk)
    return jnp.concatenate(
        [jnp.where((lane >= i * width) & (lane < (i + 1) * width), k_blk, zero) for i in range(n)], axis=0)


def _att_scratch(n_streams):
    t = ATT_BLOCK
    return [
        pltpu.VMEM((n_streams, 1, t), F32),
        pltpu.VMEM((n_streams, 1, t), F32),
        pltpu.VMEM((n_streams, HEAD_DIM, t), F32),
        pltpu.VMEM((2, n_streams * t, t), F32),
    ]


def _pipelined_kv_loop(lo, hi, scores, s_ref, step):
    s_ref[lo & 1] = scores(lo)

    def body(j, carry):
        slot = j & 1
        step(j, s_ref.at[slot])
        s_ref[1 - slot] = scores(j + 1)
        return carry

    lax.fori_loop(lo, hi, body, 0)


def _att_specs(batch_blocks, q_blk, k_blk, v_blk, seq):
    t = ATT_BLOCK
    nq = seq // t
    return [
        pl.BlockSpec((t, LANES), lambda b, hp, qi: (b * nq + qi, q_blk + hp)),
        pl.BlockSpec((seq, LANES), lambda b, hp, qi: (b, k_blk + hp)),
        pl.BlockSpec((nq, LANES, t), lambda b, hp, qi: (b, v_blk + hp, 0)),
    ]


def _diff_attn_kernel(q_ref, k_ref, vt_ref, lq1_ref, lk1_ref, lq2_ref, lk2_ref, g_ref, o_ref,
                      m_ref, l_ref, acc_ref, s_ref, *, lam_init):
    qi = pl.program_id(2)
    t = ATT_BLOCK
    lane = lax.broadcasted_iota(jnp.int32, (1, LANES), 1)
    q = q_ref[...]

    def scores(j):
        return _nt_dot(_masked_stack(_k_block(k_ref, j), lane, DIFF_DIM, 4), q)

    s_all = scores(qi)
    vt = vt_ref[qi]
    mask = _causal_mask_t(t)
    for i in range(4):
        s = jnp.where(mask, s_all[i * t:(i + 1) * t], NEG_INF)
        hh = i // 2
        _softmax_first(s, vt[hh * HEAD_DIM:(hh + 1) * HEAD_DIM], m_ref, l_ref, acc_ref, i)

    def step(j, s_slot):
        vt = vt_ref[j]
        for i in range(4):
            hh = i // 2
            _softmax_next(s_slot[i * t:(i + 1) * t, :], vt[hh * HEAD_DIM:(hh + 1) * HEAD_DIM],
                          m_ref, l_ref, acc_ref, i)

    _pipelined_kv_loop(0, qi, scores, s_ref, step)

    lam = (jnp.exp(jnp.sum(lq1_ref[...] * lk1_ref[...], axis=1, keepdims=True))
           - jnp.exp(jnp.sum(lq2_ref[...] * lk2_ref[...], axis=1, keepdims=True)) + lam_init)
    outs = []
    for hh in range(2):
        o = acc_ref[2 * hh] / l_ref[2 * hh] - lam * (acc_ref[2 * hh + 1] / l_ref[2 * hh + 1])
        ms = jnp.mean(o * o, axis=0, keepdims=True)
        outs.append(o * lax.rsqrt(ms + EPS))
    y = jnp.concatenate(outs, axis=0) * g_ref[...] * (1.0 - lam_init)
    o_ref[...] = y.T.astype(BF16)


def _diff_attn(qk, vt, lq1, lk1, lq2, lk2, g_t, batch, seq, lam_init):
    t = ATT_BLOCK
    nq = seq // t
    vec = pl.BlockSpec((1, DIFF_DIM), lambda b, hp, qi: (0, 0))
    return pl.pallas_call(
        functools.partial(_diff_attn_kernel, lam_init=lam_init),
        grid=(batch, WIDTH_A // LANES, nq),
        in_specs=_att_specs(batch, _QA_BLK, _KA_BLK, _VA_BLK, seq) + [
            vec, vec, vec, vec,
            pl.BlockSpec((LANES, t), lambda b, hp, qi: (0, 0)),
        ],
        out_specs=pl.BlockSpec((t, LANES), lambda b, hp, qi: (b * nq + qi, hp)),
        out_shape=jax.ShapeDtypeStruct((batch * seq, WIDTH_A), BF16),
        scratch_shapes=_att_scratch(4),
        compiler_params=_cparams("parallel", "parallel", "arbitrary"),
        name="diff_attn",
    )(qk, qk, vt, lq1, lk1, lq2, lk2, g_t)


def _moba_attn_kernel(q_ref, k_ref, vt_ref, o_ref, kmean_ref, bias_ref, m_ref, l_ref, acc_ref, s_ref, *, nb):
    qi = pl.program_id(2)
    t = ATT_BLOCK
    lane = lax.broadcasted_iota(jnp.int32, (1, LANES), 1)

    @pl.when(qi == 0)
    def _():
        kmean_ref[...] = jnp.zeros_like(kmean_ref)
        for j in range(nb):
            kb = k_ref[j * t:(j + 1) * t, :].astype(F32)
            kmean_ref[j:j + 1, :] = jnp.mean(kb, axis=0, keepdims=True)

    q = q_ref[...]
    kmean = kmean_ref[...].astype(BF16)
    gate_all = _nt_dot(_masked_stack(kmean, lane, HEAD_DIM, 2), q)
    blk = lax.broadcasted_iota(jnp.int32, (nb, t), 0)
    for hh in range(2):
        gate = gate_all[hh * LANES:hh * LANES + nb]
        cnt = jnp.zeros((nb, t), jnp.int32)
        for i in range(nb):
            gi = gate[i:i + 1, :]
            beats = (gi > gate) | ((gi == gate) & (blk > i))
            live = jnp.where(i < qi, 1, 0)
            cnt = cnt + jnp.where(beats, live, 0)
        sel = (cnt < MOBA_TOPK) & (blk < qi)
        bias_ref[hh] = jnp.where(sel, 0.0, NEG_INF)

    def scores(j):
        return _nt_dot(_masked_stack(_k_block(k_ref, j), lane, HEAD_DIM, 2), q)

    s_all = scores(qi)
    vt = vt_ref[qi]
    mask = _causal_mask_t(t)
    for hh in range(2):
        s = jnp.where(mask, s_all[hh * t:(hh + 1) * t], NEG_INF)
        _softmax_first(s, vt[hh * HEAD_DIM:(hh + 1) * HEAD_DIM], m_ref, l_ref, acc_ref, hh)

    def step(j, s_slot):
        vt = vt_ref[j]
        for hh in range(2):
            _softmax_next(s_slot[hh * t:(hh + 1) * t, :], vt[hh * HEAD_DIM:(hh + 1) * HEAD_DIM],
                          m_ref, l_ref, acc_ref, hh, col_bias=bias_ref[hh, pl.ds(j, 1), :])

    _pipelined_kv_loop(0, qi, scores, s_ref, step)

    o = jnp.concatenate([acc_ref[0] / l_ref[0], acc_ref[1] / l_ref[1]], axis=0)
    o_ref[...] = o.T.astype(BF16)


def _moba_attn(qk, vt, batch, seq):
    t = ATT_BLOCK
    nq = seq // t
    nb = seq // MOBA_BLOCK
    return pl.pallas_call(
        functools.partial(_moba_attn_kernel, nb=nb),
        grid=(batch, WIDTH_B // LANES, nq),
        in_specs=_att_specs(batch, _QB_BLK, _KB_BLK, _VB_BLK, seq),
        out_specs=pl.BlockSpec((t, LANES), lambda b, hp, qi: (b * nq + qi, hp)),
        out_shape=jax.ShapeDtypeStruct((batch * seq, WIDTH_B), BF16),
        scratch_shapes=[pltpu.VMEM((LANES, LANES), F32), pltpu.VMEM((2, nb, t), F32)] + _att_scratch(2),
        compiler_params=_cparams("parallel", "parallel", "arbitrary"),
        name="moba_attn",
    )(qk, qk, vt)


def _dilated_bias_table():
    t = ATT_BLOCK
    reach = max(w for w, _ in DILATED_PATTERNS)
    nd = reach // t + 1
    d = (np.arange(nd)[:, None, None] * t + np.arange(t)[None, None, :] - np.arange(t)[None, :, None])
    mult = np.zeros(d.shape, np.int32)
    for window, dil in DILATED_PATTERNS:
        mult += ((d >= 0) & (d <= window) & (d % dil == 0)).astype(np.int32)
    return np.where(mult > 0, np.log2(np.maximum(mult, 1).astype(np.float64)), NEG_INF).astype(np.float32)


def _dilated_attn_kernel(q_ref, k_ref, vt_ref, bias_ref, o_ref, m_ref, l_ref, acc_ref, s_ref, *, nd):
    qi = pl.program_id(2)
    t = ATT_BLOCK
    lane = lax.broadcasted_iota(jnp.int32, (1, LANES), 1)
    q = q_ref[...]

    def scores(j):
        return _nt_dot(_masked_stack(_k_block(k_ref, j), lane, HEAD_DIM, 2), q)

    s_all = scores(qi)
    vt = vt_ref[qi]
    for hh in range(2):
        _softmax_first(s_all[hh * t:(hh + 1) * t] + bias_ref[0], vt[hh * HEAD_DIM:(hh + 1) * HEAD_DIM],
                       m_ref, l_ref, acc_ref, hh)

    def step(j, s_slot):
        vt = vt_ref[j]
        bias = bias_ref[qi - j]
        for hh in range(2):
            _softmax_next(s_slot[hh * t:(hh + 1) * t, :] + bias, vt[hh * HEAD_DIM:(hh + 1) * HEAD_DIM],
                          m_ref, l_ref, acc_ref, hh)

    _pipelined_kv_loop(jnp.maximum(qi - (nd - 1), 0), qi, scores, s_ref, step)

    o = jnp.concatenate([acc_ref[0] / l_ref[0], acc_ref[1] / l_ref[1]], axis=0)
    o_ref[...] = o.T.astype(BF16)


def _dilated_attn(qk, vt, bias, batch, seq):
    t = ATT_BLOCK
    nq = seq // t
    nd = bias.shape[0]
    return pl.pallas_call(
        functools.partial(_dilated_attn_kernel, nd=nd),
        grid=(batch, WIDTH_C // LANES, nq),
        in_specs=_att_specs(batch, _QC_BLK, _KC_BLK, _VC_BLK, seq) + [
            pl.BlockSpec((nd, t, t), lambda b, hp, qi: (0, 0, 0)),
        ],
        out_specs=pl.BlockSpec((t, LANES), lambda b, hp, qi: (b * nq + qi, hp)),
        out_shape=jax.ShapeDtypeStruct((batch * seq, WIDTH_C), BF16),
        scratch_shapes=_att_scratch(2),
        compiler_params=_cparams("parallel", "parallel", "arbitrary"),
        name="dilated_attn",
    )(qk, qk, vt, bias)


def _proj_out_kernel(oa_ref, ob_ref, oc_ref, w_ref, x_ref, g_ref, xo_ref, h_ref):
    acc = jnp.dot(oa_ref[...], w_ref[:WIDTH_A, :], preferred_element_type=F32)
    acc = acc + jnp.dot(ob_ref[...], w_ref[WIDTH_A:WIDTH_A + WIDTH_B, :], preferred_element_type=F32)
    acc = acc + jnp.dot(oc_ref[...], w_ref[WIDTH_A + WIDTH_B:, :], preferred_element_type=F32)
    xn = x_ref[...] + acc
    xo_ref[...] = xn
    ms = jnp.mean(xn * xn, axis=-1, keepdims=True)
    h_ref[...] = (xn * lax.rsqrt(ms + EPS) * g_ref[...]).astype(BF16)


def _proj_out(oa, ob, oc, w, x2d, g):
    m = x2d.shape[0]
    tm = ROW_TILE
    row = lambda i: (i, 0)
    fixed = lambda i: (0, 0)
    return pl.pallas_call(
        _proj_out_kernel,
        grid=(m // tm,),
        in_specs=[
            pl.BlockSpec((tm, WIDTH_A), row),
            pl.BlockSpec((tm, WIDTH_B), row),
            pl.BlockSpec((tm, WIDTH_C), row),
            pl.BlockSpec((D_MODEL, D_MODEL), fixed),
            pl.BlockSpec((tm, D_MODEL), row),
            pl.BlockSpec((1, D_MODEL), fixed),
        ],
        out_specs=[pl.BlockSpec((tm, D_MODEL), row), pl.BlockSpec((tm, D_MODEL), row)],
        out_shape=[jax.ShapeDtypeStruct((m, D_MODEL), F32), jax.ShapeDtypeStruct((m, D_MODEL), BF16)],
        compiler_params=_cparams("parallel"),
        name="proj_out",
    )(oa, ob, oc, w, x2d, g)


def _mlp_kernel(h_ref, halo_ref, x_ref, wup_ref, cw_ref, cb_ref, wd_ref, gf_ref, o_ref,
                hcat_ref, u_ref, *, tiles_per_seq, final_norm):
    i = pl.program_id(0)
    tm = h_ref.shape[0]
    nf = wd_ref.shape[0]
    seq_start = (i % tiles_per_seq) == 0
    halo = halo_ref[...]
    hcat_ref[:HALO, :] = jnp.where(seq_start, jnp.zeros_like(halo), halo)
    hcat_ref[HALO:, :] = h_ref[...]
    o_ref[...] = x_ref[...]

    def up(c, slot):
        for b in range(2):
            u_ref[slot, b] = jnp.dot(hcat_ref[...], wup_ref[b * nf + c], preferred_element_type=F32)

    def down(c, slot):
        def conv(b):
            cw = cw_ref[b * nf + c]
            return (cw[2:3] * u_ref[slot, b, HALO:HALO + tm, :]
                    + cw[1:2] * u_ref[slot, b, HALO - 1:HALO - 1 + tm, :]
                    + cw[0:1] * u_ref[slot, b, HALO - 2:HALO - 2 + tm, :]
                    + cb_ref[b * nf + c])

        gate = conv(0)
        val = conv(1)
        act = (gate * jax.nn.sigmoid(gate) * val).astype(BF16)
        o_ref[...] += jnp.dot(act, wd_ref[c], preferred_element_type=F32)

    up(0, 0)

    def body(c, carry):
        slot = c & 1
        down(c, slot)
        up(c + 1, 1 - slot)
        return carry

    lax.fori_loop(0, nf - 1, body, 0)
    down(nf - 1, (nf - 1) & 1)

    if final_norm:
        y = o_ref[...]
        ms = jnp.mean(y * y, axis=-1, keepdims=True)
        o_ref[...] = y * lax.rsqrt(ms + EPS) * gf_ref[...]


def _mlp(h, x2d, w_up, conv_w, conv_b, w_down, g_final, seq, final_norm):
    m = x2d.shape[0]
    tm, tf = ROW_TILE, FF_TILE
    nf = D_FF // tf
    hb = tm // HALO
    wup3 = w_up.astype(BF16).reshape(D_MODEL, 2 * nf, tf).transpose(1, 0, 2)
    cw3 = conv_w.reshape(3, 2 * nf, tf).transpose(1, 0, 2)
    cb3 = conv_b.reshape(2 * nf, 1, tf)
    wd3 = w_down.astype(BF16).reshape(nf, tf, D_MODEL)
    row = lambda i: (i, 0)
    whole = lambda i: (0, 0, 0)
    once = pl.Buffered(1)
    return pl.pallas_call(
        functools.partial(_mlp_kernel, tiles_per_seq=seq // tm, final_norm=final_norm),
        grid=(m // tm,),
        in_specs=[
            pl.BlockSpec((tm, D_MODEL), row),
            pl.BlockSpec((HALO, D_MODEL), lambda i: (jnp.maximum(i * hb - 1, 0), 0)),
            pl.BlockSpec((tm, D_MODEL), row),
            pl.BlockSpec((2 * nf, D_MODEL, tf), whole, pipeline_mode=once),
            pl.BlockSpec((2 * nf, 3, tf), whole, pipeline_mode=once),
            pl.BlockSpec((2 * nf, 1, tf), whole, pipeline_mode=once),
            pl.BlockSpec((nf, tf, D_MODEL), whole, pipeline_mode=once),
            pl.BlockSpec((1, D_MODEL), lambda i: (0, 0)),
        ],
        out_specs=pl.BlockSpec((tm, D_MODEL), row),
        out_shape=jax.ShapeDtypeStruct((m, D_MODEL), F32),
        scratch_shapes=[
            pltpu.VMEM((HALO + tm, D_MODEL), BF16),
            pltpu.VMEM((2, 2, HALO + tm, tf), F32),
        ],
        compiler_params=_cparams("parallel"),
        name="conv_mlp",
    )(h, h, x2d, wup3, cw3, cb3, wd3, g_final)


def _split_w_in(w):
    a, b = WIDTH_A, WIDTH_B
    qk_cols = [w[:, 0:2 * a], w[:, 3 * a:3 * a + 2 * b], w[:, 3 * a + 3 * b:3 * a + 3 * b + 2 * WIDTH_C]]
    v_cols = [w[:, 2 * a:3 * a], w[:, 3 * a + 2 * b:3 * a + 3 * b], w[:, 3 * a + 3 * b + 2 * WIDTH_C:]]
    return (jnp.concatenate(qk_cols, axis=1).astype(BF16),
            jnp.concatenate(v_cols, axis=1).T.astype(BF16))


def kernel(x, positions, g_mix, w_in, w_out, lambda_q1, lambda_k1, lambda_q2, lambda_k2, g_diff, g_ffn,
           w_up, conv_w, conv_b, w_down, g_final):
    batch, seq, d_model = x.shape
    depth = g_mix.shape[0]
    assert d_model == D_MODEL and seq % max(ROW_TILE, MOBA_BLOCK, ATT_BLOCK) == 0
    m = batch * seq
    tab = _rope_tables(positions)
    bias = jnp.asarray(_dilated_bias_table())
    xf = x.reshape(m, d_model)
    for layer in range(depth):
        lam_init = 0.8 - 0.6 * math.exp(-0.3 * layer)
        wqk, wvt = _split_w_in(w_in[layer])
        qk, vt = _proj_in(xf, g_mix[layer][None, :], wqk, wvt, tab)
        g_t = jnp.broadcast_to(jnp.tile(g_diff[layer], 2)[:, None], (LANES, ATT_BLOCK))
        o_a = _diff_attn(qk, vt, lambda_q1[layer][None, :], lambda_k1[layer][None, :],
                         lambda_q2[layer][None, :], lambda_k2[layer][None, :], g_t, batch, seq, lam_init)
        o_b = _moba_attn(qk, vt, batch, seq)
        o_c = _dilated_attn(qk, vt, bias, batch, seq)
        xf, h2 = _proj_out(o_a, o_b, o_c, w_out[layer].astype(BF16), xf, g_ffn[layer][None, :])
        xf = _mlp(h2, xf, w_up[layer].astype(BF16), conv_w[layer], conv_b[layer][None, :],
                  w_down[layer].astype(BF16), g_final[None, :], seq, layer == depth - 1)
    return xf.reshape(batch, seq, d_model)
```

```python
import functools
import math

import numpy as np
import jax
import jax.numpy as jnp
from jax import lax
from jax.experimental import pallas as pl
from jax.experimental.pallas import tpu as pltpu

F32 = jnp.float32
BF16 = jnp.bfloat16

D_MODEL = 1024
HEAD_DIM = 64
DIFF_DIM = 32
WIDTH_A = 256
WIDTH_B = 256
WIDTH_C = 512
ROPE_THETA = 500000.0
MOBA_BLOCK = 256
MOBA_TOPK = 3
DILATED_PATTERNS = ((128, 1), (512, 4), (2048, 16))
D_FF = 2816
EPS = 1e-6
NEG_INF = -1e30
LOG2E = math.log2(math.e)

LANES = 128
ATT_BLOCK = 256
ROW_TILE = 512
FF_TILE = 256
HALO = 16
VMEM_LIMIT = 48 * 1024 * 1024

_QK_SEGMENTS = (
    (0, 256, "qa"), (256, 256, "ka"),
    (512, 256, "q"), (768, 256, "k"),
    (1024, 512, "q"), (1536, 512, "k"),
)
_QK_WIDTH = 2048
_V_WIDTH = 1024
_QA_BLK, _KA_BLK, _QB_BLK, _KB_BLK, _QC_BLK, _KC_BLK = 0, 2, 4, 6, 8, 12
_VA_BLK, _VB_BLK, _VC_BLK = 0, 2, 4


def _cparams(*sem):
    return pltpu.CompilerParams(dimension_semantics=sem, vmem_limit_bytes=VMEM_LIMIT)


def _rope_table_kernel(pos_ref, tab_ref):
    pos = pos_ref[...].astype(F32)
    lane = lax.broadcasted_iota(jnp.int32, (1, 2 * LANES), 1)
    is_a = lane < LANES
    d = jnp.where(is_a, lane & (DIFF_DIM - 1), lane & (HEAD_DIM - 1))
    half = jnp.where(is_a, DIFF_DIM // 8, HEAD_DIM // 8)
    rot = 2 * half
    j = jnp.where(d < half, d, d - half)
    expo = -(j.astype(F32)) * 2.0 / rot.astype(F32)
    inv = jnp.where(d < rot, jnp.power(jnp.float32(ROPE_THETA), expo), 0.0)
    sign = jnp.where(d < half, -1.0, 1.0).astype(F32)
    ang = pos * inv
    tab_ref[:, : 2 * LANES] = jnp.cos(ang)
    tab_ref[:, 2 * LANES:] = jnp.sin(ang) * sign


def _rope_tables(positions):
    m = positions.size
    tr = ROW_TILE
    return pl.pallas_call(
        _rope_table_kernel,
        grid=(m // tr,),
        in_specs=[pl.BlockSpec((tr, 1), lambda i: (i, 0))],
        out_specs=pl.BlockSpec((tr, 4 * LANES), lambda i: (i, 0)),
        out_shape=jax.ShapeDtypeStruct((m, 4 * LANES), F32),
        compiler_params=_cparams("parallel"),
        name="rope_tables",
    )(positions.reshape(m, 1))


def _nt_dot(a, b):
    return lax.dot_general(a, b, (((1,), (1,)), ((), ())), preferred_element_type=F32)


def _proj_in_kernel(x_ref, g_ref, wqk_ref, wvt_ref, tab_ref, qk_ref, vt_ref):
    x = x_ref[...]
    ms = jnp.mean(x * x, axis=-1, keepdims=True)
    h = (x * lax.rsqrt(ms + EPS) * g_ref[...]).astype(BF16)
    lane = lax.broadcasted_iota(jnp.int32, (1, LANES), 1)
    for start, width, kind in _QK_SEGMENTS:
        acc = jnp.dot(h, wqk_ref[:, start:start + width], preferred_element_type=F32)
        narrow = kind.endswith("a")
        dim = DIFF_DIM if narrow else HEAD_DIM
        half = dim // 8
        off = 0 if narrow else LANES
        cos_t = tab_ref[:, off:off + LANES]
        sin_t = tab_ref[:, 2 * LANES + off:3 * LANES + off]
        upper = (lane & (dim - 1)) >= half
        scale = dim ** -0.5 * LOG2E if kind.startswith("q") else None
        for c in range(width // LANES):
            xs = acc[:, c * LANES:(c + 1) * LANES]
            partner = jnp.where(upper, pltpu.roll(xs, half, 1), pltpu.roll(xs, LANES - half, 1))
            y = xs * cos_t + partner * sin_t
            if scale is not None:
                y = y * scale
            qk_ref[:, start + c * LANES:start + (c + 1) * LANES] = y.astype(BF16)
    t = ATT_BLOCK
    for r in range(_V_WIDTH // t):
        vt = _nt_dot(wvt_ref[r * t:(r + 1) * t, :], h).astype(BF16)
        for c in range(vt_ref.shape[0]):
            vt_ref[c, r * t:(r + 1) * t, :] = vt[:, c * t:(c + 1) * t]


def _proj_in(x2d, g, wqk, wvt, tab):
    m = x2d.shape[0]
    tm, t = ROW_TILE, ATT_BLOCK
    return pl.pallas_call(
        _proj_in_kernel,
        grid=(m // tm,),
        in_specs=[
            pl.BlockSpec((tm, D_MODEL), lambda i: (i, 0)),
            pl.BlockSpec((1, D_MODEL), lambda i: (0, 0)),
            pl.BlockSpec((D_MODEL, _QK_WIDTH), lambda i: (0, 0)),
            pl.BlockSpec((_V_WIDTH, D_MODEL), lambda i: (0, 0)),
            pl.BlockSpec((tm, 4 * LANES), lambda i: (i, 0)),
        ],
        out_specs=[
            pl.BlockSpec((tm, _QK_WIDTH), lambda i: (i, 0)),
            pl.BlockSpec((tm // t, _V_WIDTH, t), lambda i: (i, 0, 0)),
        ],
        out_shape=[
            jax.ShapeDtypeStruct((m, _QK_WIDTH), BF16),
            jax.ShapeDtypeStruct((m // t, _V_WIDTH, t), BF16),
        ],
        compiler_params=_cparams("parallel"),
        name="proj_in",
    )(x2d, g, wqk, wvt, tab)


def _softmax_first(s, vt, m_ref, l_ref, acc_ref, i):
    m = jnp.max(s, axis=0, keepdims=True)
    p = jnp.exp2(s - m)
    m_ref[i] = m
    l_ref[i] = jnp.sum(p, axis=0, keepdims=True)
    acc_ref[i] = jnp.dot(vt, p.astype(BF16), preferred_element_type=F32)


def _softmax_next(s, vt, m_ref, l_ref, acc_ref, i, col_bias=None):
    m_prev = m_ref[i]
    m_blk = jnp.max(s, axis=0, keepdims=True)
    if col_bias is not None:
        m_blk = m_blk + col_bias
    m_new = jnp.maximum(m_prev, m_blk)
    alpha = jnp.exp2(m_prev - m_new)
    shift = m_new if col_bias is None else m_new - col_bias
    p = jnp.exp2(s - shift)
    m_ref[i] = m_new
    l_ref[i] = alpha * l_ref[i] + jnp.sum(p, axis=0, keepdims=True)
    acc_ref[i] = alpha * acc_ref[i] + jnp.dot(vt, p.astype(BF16), preferred_element_type=F32)


def _causal_mask_t(t):
    key = lax.broadcasted_iota(jnp.int32, (t, t), 0)
    qry = lax.broadcasted_iota(jnp.int32, (t, t), 1)
    return key <= qry


def _k_block(ref, j):
    return ref[pl.ds(pl.multiple_of(j * ATT_BLOCK, ATT_BLOCK), ATT_BLOCK), :]


def _masked_stack(k_blk, lane, width, n):
    zero = jnp.zeros_like(k_blk)
    return jnp.concatenate(
        [jnp.where((lane >= i * width) & (lane < (i + 1) * width), k_blk, zero) for i in range(n)], axis=0)


def _att_scratch(n_streams):
    t = ATT_BLOCK
    return [
        pltpu.VMEM((n_streams, 1, t), F32),
        pltpu.VMEM((n_streams, 1, t), F32),
        pltpu.VMEM((n_streams, HEAD_DIM, t), F32),
        pltpu.VMEM((n_streams * t, t), F32),
        pltpu.VMEM((n_streams * t, t), F32),
    ]


def _pipelined_kv_loop(lo, hi, scores, s_a, s_b, step):
    n = hi - lo

    def body(i, carry):
        j = lo + 2 * i
        s_b[...] = scores(j + 1)
        step(j, s_a)
        s_a[...] = scores(j + 2)
        step(j + 1, s_b)
        return carry

    lax.fori_loop(0, n // 2, body, 0)

    @pl.when(n % 2 == 1)
    def _():
        step(hi - 1, s_a)


def _att_specs(batch_blocks, q_blk, k_blk, v_blk, seq):
    t = ATT_BLOCK
    nq = seq // t
    return [
        pl.BlockSpec((t, LANES), lambda b, hp, qi: (b * nq + qi, q_blk + hp)),
        pl.BlockSpec((seq, LANES), lambda b, hp, qi: (b, k_blk + hp)),
        pl.BlockSpec((nq, LANES, t), lambda b, hp, qi: (b, v_blk + hp, 0)),
    ]


def _diff_attn_kernel(q_ref, k_ref, vt_ref, lq1_ref, lk1_ref, lq2_ref, lk2_ref, g_ref, o_ref,
                      m_ref, l_ref, acc_ref, s_a, s_b, *, lam_init):
    qi = pl.program_id(2)
    t = ATT_BLOCK
    lane = lax.broadcasted_iota(jnp.int32, (1, LANES), 1)
    q = q_ref[...]

    def scores(j):
        return _nt_dot(_masked_stack(_k_block(k_ref, j), lane, DIFF_DIM, 4), q)

    s_all = scores(qi)
    s_a[...] = scores(0)
    vt = vt_ref[qi]
    mask = _causal_mask_t(t)
    for i in range(4):
        s = jnp.where(mask, s_all[i * t:(i + 1) * t], NEG_INF)
        hh = i // 2
        _softmax_first(s, vt[hh * HEAD_DIM:(hh + 1) * HEAD_DIM], m_ref, l_ref, acc_ref, i)

    def step(j, s_slot):
        vt = vt_ref[j]
        for i in range(4):
            hh = i // 2
            _softmax_next(s_slot[i * t:(i + 1) * t, :], vt[hh * HEAD_DIM:(hh + 1) * HEAD_DIM],
                          m_ref, l_ref, acc_ref, i)

    _pipelined_kv_loop(0, qi, scores, s_a, s_b, step)

    lam = (jnp.exp(jnp.sum(lq1_ref[...] * lk1_ref[...], axis=1, keepdims=True))
           - jnp.exp(jnp.sum(lq2_ref[...] * lk2_ref[...], axis=1, keepdims=True)) + lam_init)
    outs = []
    for hh in range(2):
        o = acc_ref[2 * hh] / l_ref[2 * hh] - lam * (acc_ref[2 * hh + 1] / l_ref[2 * hh + 1])
        ms = jnp.mean(o * o, axis=0, keepdims=True)
        outs.append(o * lax.rsqrt(ms + EPS))
    y = jnp.concatenate(outs, axis=0) * g_ref[...] * (1.0 - lam_init)
    o_ref[...] = y.T.astype(BF16)


def _diff_attn(qk, vt, lq1, lk1, lq2, lk2, g_t, batch, seq, lam_init):
    t = ATT_BLOCK
    nq = seq // t
    vec = pl.BlockSpec((1, DIFF_DIM), lambda b, hp, qi: (0, 0))
    return pl.pallas_call(
        functools.partial(_diff_attn_kernel, lam_init=lam_init),
        grid=(batch, WIDTH_A // LANES, nq),
        in_specs=_att_specs(batch, _QA_BLK, _KA_BLK, _VA_BLK, seq) + [
            vec, vec, vec, vec,
            pl.BlockSpec((LANES, t), lambda b, hp, qi: (0, 0)),
        ],
        out_specs=pl.BlockSpec((t, LANES), lambda b, hp, qi: (b * nq + qi, hp)),
        out_shape=jax.ShapeDtypeStruct((batch * seq, WIDTH_A), BF16),
        scratch_shapes=_att_scratch(4),
        compiler_params=_cparams("parallel", "parallel", "arbitrary"),
        name="diff_attn",
    )(qk, qk, vt, lq1, lk1, lq2, lk2, g_t)


def _moba_attn_kernel(q_ref, k_ref, vt_ref, o_ref, kmean_ref, bias_ref, m_ref, l_ref, acc_ref, s_a, s_b, *, nb):
    qi = pl.program_id(2)
    t = ATT_BLOCK
    lane = lax.broadcasted_iota(jnp.int32, (1, LANES), 1)

    @pl.when(qi == 0)
    def _():
        kmean_ref[...] = jnp.zeros_like(kmean_ref)
        for j in range(nb):
            kb = k_ref[j * t:(j + 1) * t, :].astype(F32)
            kmean_ref[j:j + 1, :] = jnp.mean(kb, axis=0, keepdims=True)

    q = q_ref[...]
    kmean = kmean_ref[...].astype(BF16)
    gate_all = _nt_dot(_masked_stack(kmean, lane, HEAD_DIM, 2), q)
    blk = lax.broadcasted_iota(jnp.int32, (nb, t), 0)
    for hh in range(2):
        gate = gate_all[hh * LANES:hh * LANES + nb]
        cnt = jnp.zeros((nb, t), jnp.int32)
        for i in range(nb):
            gi = gate[i:i + 1, :]
            beats = (gi > gate) | ((gi == gate) & (blk > i))
            live = jnp.where(i < qi, 1, 0)
            cnt = cnt + jnp.where(beats, live, 0)
        sel = (cnt < MOBA_TOPK) & (blk < qi)
        bias_ref[hh] = jnp.where(sel, 0.0, NEG_INF)

    def scores(j):
        return _nt_dot(_masked_stack(_k_block(k_ref, j), lane, HEAD_DIM, 2), q)

    s_all = scores(qi)
    s_a[...] = scores(0)
    vt = vt_ref[qi]
    mask = _causal_mask_t(t)
    for hh in range(2):
        s = jnp.where(mask, s_all[hh * t:(hh + 1) * t], NEG_INF)
        _softmax_first(s, vt[hh * HEAD_DIM:(hh + 1) * HEAD_DIM], m_ref, l_ref, acc_ref, hh)

    def step(j, s_slot):
        vt = vt_ref[j]
        for hh in range(2):
            _softmax_next(s_slot[hh * t:(hh + 1) * t, :], vt[hh * HEAD_DIM:(hh + 1) * HEAD_DIM],
                          m_ref, l_ref, acc_ref, hh, col_bias=bias_ref[hh, pl.ds(j, 1), :])

    _pipelined_kv_loop(0, qi, scores, s_a, s_b, step)

    o = jnp.concatenate([acc_ref[0] / l_ref[0], acc_ref[1] / l_ref[1]], axis=0)
    o_ref[...] = o.T.astype(BF16)


def _moba_attn(qk, vt, batch, seq):
    t = ATT_BLOCK
    nq = seq // t
    nb = seq // MOBA_BLOCK
    return pl.pallas_call(
        functools.partial(_moba_attn_kernel, nb=nb),
        grid=(batch, WIDTH_B // LANES, nq),
        in_specs=_att_specs(batch, _QB_BLK, _KB_BLK, _VB_BLK, seq),
        out_specs=pl.BlockSpec((t, LANES), lambda b, hp, qi: (b * nq + qi, hp)),
        out_shape=jax.ShapeDtypeStruct((batch * seq, WIDTH_B), BF16),
        scratch_shapes=[pltpu.VMEM((LANES, LANES), F32), pltpu.VMEM((2, nb, t), F32)] + _att_scratch(2),
        compiler_params=_cparams("parallel", "parallel", "arbitrary"),
        name="moba_attn",
    )(qk, qk, vt)


def _dilated_bias_table():
    t = ATT_BLOCK
    reach = max(w for w, _ in DILATED_PATTERNS)
    nd = reach // t + 1
    d = (np.arange(nd)[:, None, None] * t + np.arange(t)[None, None, :] - np.arange(t)[None, :, None])
    mult = np.zeros(d.shape, np.int32)
    for window, dil in DILATED_PATTERNS:
        mult += ((d >= 0) & (d <= window) & (d % dil == 0)).astype(np.int32)
    return np.where(mult > 0, np.log2(np.maximum(mult, 1).astype(np.float64)), NEG_INF).astype(np.float32)


def _dilated_attn_kernel(q_ref, k_ref, vt_ref, bias_ref, o_ref, m_ref, l_ref, acc_ref, s_a, s_b, *, nd):
    qi = pl.program_id(2)
    t = ATT_BLOCK
    lane = lax.broadcasted_iota(jnp.int32, (1, LANES), 1)
    q = q_ref[...]

    def scores(j):
        return _nt_dot(_masked_stack(_k_block(k_ref, j), lane, HEAD_DIM, 2), q)

    lo = jnp.maximum(qi - (nd - 1), 0)
    s_all = scores(qi)
    s_a[...] = scores(lo)
    vt = vt_ref[qi]
    for hh in range(2):
        _softmax_first(s_all[hh * t:(hh + 1) * t] + bias_ref[0], vt[hh * HEAD_DIM:(hh + 1) * HEAD_DIM],
                       m_ref, l_ref, acc_ref, hh)

    def step(j, s_slot):
        vt = vt_ref[j]
        bias = bias_ref[qi - j]
        for hh in range(2):
            _softmax_next(s_slot[hh * t:(hh + 1) * t, :] + bias, vt[hh * HEAD_DIM:(hh + 1) * HEAD_DIM],
                          m_ref, l_ref, acc_ref, hh)

    _pipelined_kv_loop(lo, qi, scores, s_a, s_b, step)

    o = jnp.concatenate([acc_ref[0] / l_ref[0], acc_ref[1] / l_ref[1]], axis=0)
    o_ref[...] = o.T.astype(BF16)


def _dilated_attn(qk, vt, bias, batch, seq):
    t = ATT_BLOCK
    nq = seq // t
    nd = bias.shape[0]
    return pl.pallas_call(
        functools.partial(_dilated_attn_kernel, nd=nd),
        grid=(batch, WIDTH_C // LANES, nq),
        in_specs=_att_specs(batch, _QC_BLK, _KC_BLK, _VC_BLK, seq) + [
            pl.BlockSpec((nd, t, t), lambda b, hp, qi: (0, 0, 0)),
        ],
        out_specs=pl.BlockSpec((t, LANES), lambda b, hp, qi: (b * nq + qi, hp)),
        out_shape=jax.ShapeDtypeStruct((batch * seq, WIDTH_C), BF16),
        scratch_shapes=_att_scratch(2),
        compiler_params=_cparams("parallel", "parallel", "arbitrary"),
        name="dilated_attn",
    )(qk, qk, vt, bias)


def _proj_out_kernel(oa_ref, ob_ref, oc_ref, w_ref, x_ref, g_ref, xo_ref, h_ref):
    acc = jnp.dot(oa_ref[...], w_ref[:WIDTH_A, :], preferred_element_type=F32)
    acc = acc + jnp.dot(ob_ref[...], w_ref[WIDTH_A:WIDTH_A + WIDTH_B, :], preferred_element_type=F32)
    acc = acc + jnp.dot(oc_ref[...], w_ref[WIDTH_A + WIDTH_B:, :], preferred_element_type=F32)
    xn = x_ref[...] + acc
    xo_ref[...] = xn
    ms = jnp.mean(xn * xn, axis=-1, keepdims=True)
    h_ref[...] = (xn * lax.rsqrt(ms + EPS) * g_ref[...]).astype(BF16)


def _proj_out(oa, ob, oc, w, x2d, g):
    m = x2d.shape[0]
    tm = ROW_TILE
    row = lambda i: (i, 0)
    fixed = lambda i: (0, 0)
    return pl.pallas_call(
        _proj_out_kernel,
        grid=(m // tm,),
        in_specs=[
            pl.BlockSpec((tm, WIDTH_A), row),
            pl.BlockSpec((tm, WIDTH_B), row),
            pl.BlockSpec((tm, WIDTH_C), row),
            pl.BlockSpec((D_MODEL, D_MODEL), fixed),
            pl.BlockSpec((tm, D_MODEL), row),
            pl.BlockSpec((1, D_MODEL), fixed),
        ],
        out_specs=[pl.BlockSpec((tm, D_MODEL), row), pl.BlockSpec((tm, D_MODEL), row)],
        out_shape=[jax.ShapeDtypeStruct((m, D_MODEL), F32), jax.ShapeDtypeStruct((m, D_MODEL), BF16)],
        compiler_params=_cparams("parallel"),
        name="proj_out",
    )(oa, ob, oc, w, x2d, g)


def _mlp_kernel(h_ref, halo_ref, x_ref, wup_ref, cw_ref, cb_ref, wd_ref, gf_ref, o_ref,
                hcat_ref, u_ref, *, tiles_per_seq, final_norm):
    i = pl.program_id(0)
    tm = h_ref.shape[0]
    nf = wd_ref.shape[0]
    seq_start = (i % tiles_per_seq) == 0
    halo = halo_ref[...]
    hcat_ref[:HALO, :] = jnp.where(seq_start, jnp.zeros_like(halo), halo)
    hcat_ref[HALO:, :] = h_ref[...]
    o_ref[...] = x_ref[...]

    def up(c, slot):
        for b in range(2):
            u_ref[slot, b] = jnp.dot(hcat_ref[...], wup_ref[b * nf + c], preferred_element_type=F32)

    def down(c, slot):
        def conv(b):
            cw = cw_ref[b * nf + c]
            return (cw[2:3] * u_ref[slot, b, HALO:HALO + tm, :]
                    + cw[1:2] * u_ref[slot, b, HALO - 1:HALO - 1 + tm, :]
                    + cw[0:1] * u_ref[slot, b, HALO - 2:HALO - 2 + tm, :]
                    + cb_ref[b * nf + c])

        gate = conv(0)
        val = conv(1)
        act = (gate * jax.nn.sigmoid(gate) * val).astype(BF16)
        o_ref[...] += jnp.dot(act, wd_ref[c], preferred_element_type=F32)

    assert nf % 2 == 1
    up(0, 0)

    def body(i, carry):
        c = 2 * i
        up(c + 1, 1)
        down(c, 0)
        up(c + 2, 0)
        down(c + 1, 1)
        return carry

    lax.fori_loop(0, nf // 2, body, 0)
    down(nf - 1, 0)

    if final_norm:
        y = o_ref[...]
        ms = jnp.mean(y * y, axis=-1, keepdims=True)
        o_ref[...] = y * lax.rsqrt(ms + EPS) * gf_ref[...]


def _mlp(h, x2d, w_up, conv_w, conv_b, w_down, g_final, seq, final_norm):
    m = x2d.shape[0]
    tm, tf = ROW_TILE, FF_TILE
    nf = D_FF // tf
    hb = tm // HALO
    wup3 = w_up.astype(BF16).reshape(D_MODEL, 2 * nf, tf).transpose(1, 0, 2)
    cw3 = conv_w.reshape(3, 2 * nf, tf).transpose(1, 0, 2)
    cb3 = conv_b.reshape(2 * nf, 1, tf)
    wd3 = w_down.astype(BF16).reshape(nf, tf, D_MODEL)
    row = lambda i: (i, 0)
    whole = lambda i: (0, 0, 0)
    once = pl.Buffered(1)
    return pl.pallas_call(
        functools.partial(_mlp_kernel, tiles_per_seq=seq // tm, final_norm=final_norm),
        grid=(m // tm,),
        in_specs=[
            pl.BlockSpec((tm, D_MODEL), row),
            pl.BlockSpec((HALO, D_MODEL), lambda i: (jnp.maximum(i * hb - 1, 0), 0)),
            pl.BlockSpec((tm, D_MODEL), row),
            pl.BlockSpec((2 * nf, D_MODEL, tf), whole, pipeline_mode=once),
            pl.BlockSpec((2 * nf, 3, tf), whole, pipeline_mode=once),
            pl.BlockSpec((2 * nf, 1, tf), whole, pipeline_mode=once),
            pl.BlockSpec((nf, tf, D_MODEL), whole, pipeline_mode=once),
            pl.BlockSpec((1, D_MODEL), lambda i: (0, 0)),
        ],
        out_specs=pl.BlockSpec((tm, D_MODEL), row),
        out_shape=jax.ShapeDtypeStruct((m, D_MODEL), F32),
        scratch_shapes=[
            pltpu.VMEM((HALO + tm, D_MODEL), BF16),
            pltpu.VMEM((2, 2, HALO + tm, tf), F32),
        ],
        compiler_params=_cparams("parallel"),
        name="conv_mlp",
    )(h, h, x2d, wup3, cw3, cb3, wd3, g_final)


def _split_w_in(w):
    a, b = WIDTH_A, WIDTH_B
    qk_cols = [w[:, 0:2 * a], w[:, 3 * a:3 * a + 2 * b], w[:, 3 * a + 3 * b:3 * a + 3 * b + 2 * WIDTH_C]]
    v_cols = [w[:, 2 * a:3 * a], w[:, 3 * a + 2 * b:3 * a + 3 * b], w[:, 3 * a + 3 * b + 2 * WIDTH_C:]]
    return (jnp.concatenate(qk_cols, axis=1).astype(BF16),
            jnp.concatenate(v_cols, axis=1).T.astype(BF16))


def kernel(x, positions, g_mix, w_in, w_out, lambda_q1, lambda_k1, lambda_q2, lambda_k2, g_diff, g_ffn,
           w_up, conv_w, conv_b, w_down, g_final):
    batch, seq, d_model = x.shape
    depth = g_mix.shape[0]
    assert d_model == D_MODEL and seq % max(ROW_TILE, MOBA_BLOCK, ATT_BLOCK) == 0
    m = batch * seq
    tab = _rope_tables(positions)
    bias = jnp.asarray(_dilated_bias_table())
    xf = x.reshape(m, d_model)
    for layer in range(depth):
        lam_init = 0.8 - 0.6 * math.exp(-0.3 * layer)
        wqk, wvt = _split_w_in(w_in[layer])
        qk, vt = _proj_in(xf, g_mix[layer][None, :], wqk, wvt, tab)
        g_t = jnp.broadcast_to(jnp.tile(g_diff[layer], 2)[:, None], (LANES, ATT_BLOCK))
        o_a = _diff_attn(qk, vt, lambda_q1[layer][None, :], lambda_k1[layer][None, :],
                         lambda_q2[layer][None, :], lambda_k2[layer][None, :], g_t, batch, seq, lam_init)
        o_b = _moba_attn(qk, vt, batch, seq)
        o_c = _dilated_attn(qk, vt, bias, batch, seq)
        xf, h2 = _proj_out(o_a, o_b, o_c, w_out[layer].astype(BF16), xf, g_ffn[layer][None, :])
        xf = _mlp(h2, xf, w_up[layer].astype(BF16), conv_w[layer], conv_b[layer][None, :],
                  w_down[layer].astype(BF16), g_final[None, :], seq, layer == depth - 1)
    return xf.reshape(batch, seq, d_model)
```

```python
import functools
import math

import numpy as np
import jax
import jax.numpy as jnp
from jax import lax
from jax.experimental import pallas as pl
from jax.experimental.pallas import tpu as pltpu

F32 = jnp.float32
BF16 = jnp.bfloat16

D_MODEL = 1024
HEAD_DIM = 64
DIFF_DIM = 32
WIDTH_A = 256
WIDTH_B = 256
WIDTH_C = 512
ROPE_THETA = 500000.0
MOBA_BLOCK = 256
MOBA_TOPK = 3
DILATED_PATTERNS = ((128, 1), (512, 4), (2048, 16))
D_FF = 2816
EPS = 1e-6
NEG_INF = -1e30
LOG2E = math.log2(math.e)

LANES = 128
ATT_BLOCK = 256
DIL_TILE = 2048
ROW_TILE = 512
FF_TILE = 256
HALO = 16
VMEM_LIMIT = 48 * 1024 * 1024

_QK_SEGMENTS = (
    (0, 256, "qa"), (256, 256, "ka"),
    (512, 256, "q"), (768, 256, "k"),
    (1024, 512, "q"), (1536, 512, "k"),
)
_QK_WIDTH = 2048
_V_WIDTH = 1024
_QA_BLK, _KA_BLK, _QB_BLK, _KB_BLK, _QC_BLK, _KC_BLK = 0, 2, 4, 6, 8, 12
_VA_BLK, _VB_BLK, _VC_BLK = 0, 2, 4


def _cparams(*sem):
    return pltpu.CompilerParams(dimension_semantics=sem, vmem_limit_bytes=VMEM_LIMIT)


def _rope_table_kernel(pos_ref, tab_ref):
    pos = pos_ref[...].astype(F32)
    lane = lax.broadcasted_iota(jnp.int32, (1, 2 * LANES), 1)
    is_a = lane < LANES
    d = jnp.where(is_a, lane & (DIFF_DIM - 1), lane & (HEAD_DIM - 1))
    half = jnp.where(is_a, DIFF_DIM // 8, HEAD_DIM // 8)
    rot = 2 * half
    j = jnp.where(d < half, d, d - half)
    expo = -(j.astype(F32)) * 2.0 / rot.astype(F32)
    inv = jnp.where(d < rot, jnp.power(jnp.float32(ROPE_THETA), expo), 0.0)
    sign = jnp.where(d < half, -1.0, 1.0).astype(F32)
    ang = pos * inv
    tab_ref[:, : 2 * LANES] = jnp.cos(ang)
    tab_ref[:, 2 * LANES:] = jnp.sin(ang) * sign


def _rope_tables(positions):
    m = positions.size
    tr = ROW_TILE
    return pl.pallas_call(
        _rope_table_kernel,
        grid=(m // tr,),
        in_specs=[pl.BlockSpec((tr, 1), lambda i: (i, 0))],
        out_specs=pl.BlockSpec((tr, 4 * LANES), lambda i: (i, 0)),
        out_shape=jax.ShapeDtypeStruct((m, 4 * LANES), F32),
        compiler_params=_cparams("parallel"),
        name="rope_tables",
    )(positions.reshape(m, 1))


def _nt_dot(a, b):
    return lax.dot_general(a, b, (((1,), (1,)), ((), ())), preferred_element_type=F32)


def _proj_in_kernel(x_ref, g_ref, wqk_ref, wvt_ref, tab_ref, qk_ref, vt_ref, qk4_ref, vt4_ref, qkv16_ref,
                    cs_ref):
    tm = x_ref.shape[0]
    x = x_ref[...]
    ms = jnp.mean(x * x, axis=-1, keepdims=True)
    h = (x * lax.rsqrt(ms + EPS) * g_ref[...]).astype(BF16)
    lane = lax.broadcasted_iota(jnp.int32, (1, LANES), 1)
    slab = 0
    for start, width, kind in _QK_SEGMENTS:
        acc = jnp.dot(h, wqk_ref[:, start:start + width], preferred_element_type=F32)
        narrow = kind.endswith("a")
        dim = DIFF_DIM if narrow else HEAD_DIM
        half = dim // 8
        off = 0 if narrow else LANES
        cos_t = tab_ref[:, off:off + LANES]
        sin_t = tab_ref[:, 2 * LANES + off:3 * LANES + off]
        upper = (lane & (dim - 1)) >= half
        scale = dim ** -0.5 * LOG2E if kind.startswith("q") else None
        for c in range(width // LANES):
            xs = acc[:, c * LANES:(c + 1) * LANES]
            partner = jnp.where(upper, pltpu.roll(xs, half, 1), pltpu.roll(xs, LANES - half, 1))
            y = xs * cos_t + partner * sin_t
            if scale is not None:
                y = y * scale
            qk_ref[:, start + c * LANES:start + (c + 1) * LANES] = y.astype(BF16)
            if start >= _QC_BLK * LANES:
                cs_ref[slab] = y
                slab += 1
    t = ATT_BLOCK
    for r in range(_V_WIDTH // t):
        vt = _nt_dot(wvt_ref[r * t:(r + 1) * t, :], h)
        vtb = vt.astype(BF16)
        for c in range(vt_ref.shape[0]):
            vt_ref[c, r * t:(r + 1) * t, :] = vtb[:, c * t:(c + 1) * t]
        if r * t >= _VC_BLK * LANES:
            v_nat = vt.T
            for c in range(t // LANES):
                cs_ref[slab] = v_nat[:, c * LANES:(c + 1) * LANES]
                slab += 1
    n_qk = 2 * WIDTH_C // LANES
    n_all = 3 * WIDTH_C // LANES
    assert slab == n_all
    quarter = tm // 4
    for s in range(n_all):
        rows = [cs_ref[s, pl.ds(r, quarter, stride=4), :] for r in range(4)]
        if s < n_qk:
            for r in range(4):
                qk4_ref[r * quarter:(r + 1) * quarter, s * LANES:(s + 1) * LANES] = rows[r].astype(BF16)
        else:
            v4t = jnp.concatenate(rows, axis=0).T.astype(BF16)
            vs = s - n_qk
            for c in range(vt4_ref.shape[0]):
                vt4_ref[c, vs * LANES:(vs + 1) * LANES, :] = v4t[:, c * t:(c + 1) * t]
    per = tm // 16
    for s in range(n_all):
        for r in range(16):
            qkv16_ref[0, r, 0, :, s * LANES:(s + 1) * LANES] = (
                cs_ref[s, pl.ds(r, per, stride=16), :].astype(BF16))


def _proj_in(x2d, g, wqk, wvt, tab):
    m = x2d.shape[0]
    tm, t = ROW_TILE, ATT_BLOCK
    assert tm == DIL_TILE // 4 and tm == 4 * LANES
    per = tm // 16
    n_all = 3 * WIDTH_C // LANES
    outs = pl.pallas_call(
        _proj_in_kernel,
        grid=(m // tm,),
        in_specs=[
            pl.BlockSpec((tm, D_MODEL), lambda i: (i, 0)),
            pl.BlockSpec((1, D_MODEL), lambda i: (0, 0)),
            pl.BlockSpec((D_MODEL, _QK_WIDTH), lambda i: (0, 0)),
            pl.BlockSpec((_V_WIDTH, D_MODEL), lambda i: (0, 0)),
            pl.BlockSpec((tm, 4 * LANES), lambda i: (i, 0)),
        ],
        out_specs=[
            pl.BlockSpec((tm, _QK_WIDTH), lambda i: (i, 0)),
            pl.BlockSpec((tm // t, _V_WIDTH, t), lambda i: (i, 0, 0)),
            pl.BlockSpec((tm, 2 * WIDTH_C), lambda i: (i, 0)),
            pl.BlockSpec((tm // t, WIDTH_C, t), lambda i: (i, 0, 0)),
            pl.BlockSpec((1, 16, 1, per, 3 * WIDTH_C), lambda i: (i // 4, 0, i % 4, 0, 0)),
        ],
        out_shape=[
            jax.ShapeDtypeStruct((m, _QK_WIDTH), BF16),
            jax.ShapeDtypeStruct((m // t, _V_WIDTH, t), BF16),
            jax.ShapeDtypeStruct((m, 2 * WIDTH_C), BF16),
            jax.ShapeDtypeStruct((m // t, WIDTH_C, t), BF16),
            jax.ShapeDtypeStruct((m // DIL_TILE, 16, 4, per, 3 * WIDTH_C), BF16),
        ],
        scratch_shapes=[pltpu.VMEM((n_all, tm, LANES), F32)],
        compiler_params=_cparams("parallel"),
        name="proj_in",
    )(x2d, g, wqk, wvt, tab)
    qk, vt, qk4, vt4, qkv16 = outs
    return qk, vt, qk4, vt4, qkv16.reshape(m, 3 * WIDTH_C)


def _softmax_first(s, vt, m_ref, l_ref, acc_ref, i):
    m = jnp.max(s, axis=0, keepdims=True)
    p = jnp.exp2(s - m)
    m_ref[i] = m
    l_ref[i] = jnp.sum(p, axis=0, keepdims=True)
    acc_ref[i] = jnp.dot(vt, p.astype(BF16), preferred_element_type=F32)


def _softmax_next(s, vt, m_ref, l_ref, acc_ref, i, col_bias=None):
    m_prev = m_ref[i]
    m_blk = jnp.max(s, axis=0, keepdims=True)
    if col_bias is not None:
        m_blk = m_blk + col_bias
    m_new = jnp.maximum(m_prev, m_blk)
    alpha = jnp.exp2(m_prev - m_new)
    shift = m_new if col_bias is None else m_new - col_bias
    p = jnp.exp2(s - shift)
    m_ref[i] = m_new
    l_ref[i] = alpha * l_ref[i] + jnp.sum(p, axis=0, keepdims=True)
    acc_ref[i] = alpha * acc_ref[i] + jnp.dot(vt, p.astype(BF16), preferred_element_type=F32)


def _causal_mask_t(t):
    key = lax.broadcasted_iota(jnp.int32, (t, t), 0)
    qry = lax.broadcasted_iota(jnp.int32, (t, t), 1)
    return key <= qry


def _k_block(ref, j):
    return ref[pl.ds(pl.multiple_of(j * ATT_BLOCK, ATT_BLOCK), ATT_BLOCK), :]


def _masked_stack(k_blk, lane, width, n):
    zero = jnp.zeros_like(k_blk)
    return jnp.concatenate(
        [jnp.where((lane >= i * width) & (lane < (i + 1) * width), k_blk, zero) for i in range(n)], axis=0)


def _att_scratch(n_streams):
    t = ATT_BLOCK
    return [
        pltpu.VMEM((n_streams, 1, t), F32),
        pltpu.VMEM((n_streams, 1, t), F32),
        pltpu.VMEM((n_streams, HEAD_DIM, t), F32),
        pltpu.VMEM((n_streams * t, t), F32),
        pltpu.VMEM((n_streams * t, t), F32),
    ]


def _pipelined_kv_loop(lo, hi, scores, s_a, s_b, step):
    n = hi - lo

    def body(i, carry):
        j = lo + 2 * i
        s_b[...] = scores(j + 1)
        step(j, s_a)
        s_a[...] = scores(j + 2)
        step(j + 1, s_b)
        return carry

    lax.fori_loop(0, n // 2, body, 0)

    @pl.when(n % 2 == 1)
    def _():
        step(hi - 1, s_a)


def _att_specs(batch_blocks, q_blk, k_blk, v_blk, seq):
    t = ATT_BLOCK
    nq = seq // t
    return [
        pl.BlockSpec((t, LANES), lambda b, hp, qi: (b * nq + qi, q_blk + hp)),
        pl.BlockSpec((seq, LANES), lambda b, hp, qi: (b, k_blk + hp)),
        pl.BlockSpec((nq, LANES, t), lambda b, hp, qi: (b, v_blk + hp, 0)),
    ]


def _diff_attn_kernel(q_ref, k_ref, vt_ref, lq1_ref, lk1_ref, lq2_ref, lk2_ref, g_ref, o_ref,
                      m_ref, l_ref, acc_ref, s_a, s_b, *, lam_init):
    qi = pl.program_id(2)
    t = ATT_BLOCK
    lane = lax.broadcasted_iota(jnp.int32, (1, LANES), 1)
    q = q_ref[...]

    def scores(j):
        return _nt_dot(_masked_stack(_k_block(k_ref, j), lane, DIFF_DIM, 4), q)

    s_all = scores(qi)
    s_a[...] = scores(0)
    vt = vt_ref[qi]
    mask = _causal_mask_t(t)
    for i in range(4):
        s = jnp.where(mask, s_all[i * t:(i + 1) * t], NEG_INF)
        hh = i // 2
        _softmax_first(s, vt[hh * HEAD_DIM:(hh + 1) * HEAD_DIM], m_ref, l_ref, acc_ref, i)

    def step(j, s_slot):
        vt = vt_ref[j]
        for i in range(4):
            hh = i // 2
            _softmax_next(s_slot[i * t:(i + 1) * t, :], vt[hh * HEAD_DIM:(hh + 1) * HEAD_DIM],
                          m_ref, l_ref, acc_ref, i)

    _pipelined_kv_loop(0, qi, scores, s_a, s_b, step)

    lam = (jnp.exp(jnp.sum(lq1_ref[...] * lk1_ref[...], axis=1, keepdims=True))
           - jnp.exp(jnp.sum(lq2_ref[...] * lk2_ref[...], axis=1, keepdims=True)) + lam_init)
    outs = []
    for hh in range(2):
        o = acc_ref[2 * hh] / l_ref[2 * hh] - lam * (acc_ref[2 * hh + 1] / l_ref[2 * hh + 1])
        ms = jnp.mean(o * o, axis=0, keepdims=True)
        outs.append(o * lax.rsqrt(ms + EPS))
    y = jnp.concatenate(outs, axis=0) * g_ref[...] * (1.0 - lam_init)
    o_ref[...] = y.T.astype(BF16)


def _diff_attn(qk, vt, lq1, lk1, lq2, lk2, g_t, batch, seq, lam_init):
    t = ATT_BLOCK
    nq = seq // t
    vec = pl.BlockSpec((1, DIFF_DIM), lambda b, hp, qi: (0, 0))
    return pl.pallas_call(
        functools.partial(_diff_attn_kernel, lam_init=lam_init),
        grid=(batch, WIDTH_A // LANES, nq),
        in_specs=_att_specs(batch, _QA_BLK, _KA_BLK, _VA_BLK, seq) + [
            vec, vec, vec, vec,
            pl.BlockSpec((LANES, t), lambda b, hp, qi: (0, 0)),
        ],
        out_specs=pl.BlockSpec((t, LANES), lambda b, hp, qi: (b * nq + qi, hp)),
        out_shape=jax.ShapeDtypeStruct((batch * seq, WIDTH_A), BF16),
        scratch_shapes=_att_scratch(4),
        compiler_params=_cparams("parallel", "parallel", "arbitrary"),
        name="diff_attn",
    )(qk, qk, vt, lq1, lk1, lq2, lk2, g_t)


def _moba_attn_kernel(q_ref, k_ref, vt_ref, o_ref, kmean_ref, bias_ref, m_ref, l_ref, acc_ref, s_a, s_b, *, nb):
    qi = pl.program_id(2)
    t = ATT_BLOCK
    lane = lax.broadcasted_iota(jnp.int32, (1, LANES), 1)

    @pl.when(qi == 0)
    def _():
        kmean_ref[...] = jnp.zeros_like(kmean_ref)
        for j in range(nb):
            kb = k_ref[j * t:(j + 1) * t, :].astype(F32)
            kmean_ref[j:j + 1, :] = jnp.mean(kb, axis=0, keepdims=True)

    q = q_ref[...]
    kmean = kmean_ref[...].astype(BF16)
    gate_all = _nt_dot(_masked_stack(kmean, lane, HEAD_DIM, 2), q)
    blk = lax.broadcasted_iota(jnp.int32, (nb, t), 0)
    for hh in range(2):
        gate = gate_all[hh * LANES:hh * LANES + nb]
        cnt = jnp.zeros((nb, t), jnp.int32)
        for i in range(nb):
            gi = gate[i:i + 1, :]
            beats = (gi > gate) | ((gi == gate) & (blk > i))
            live = jnp.where(i < qi, 1, 0)
            cnt = cnt + jnp.where(beats, live, 0)
        sel = (cnt < MOBA_TOPK) & (blk < qi)
        bias_ref[hh] = jnp.where(sel, 0.0, NEG_INF)

    def scores(j):
        return _nt_dot(_masked_stack(_k_block(k_ref, j), lane, HEAD_DIM, 2), q)

    s_all = scores(qi)
    s_a[...] = scores(0)
    vt = vt_ref[qi]
    mask = _causal_mask_t(t)
    for hh in range(2):
        s = jnp.where(mask, s_all[hh * t:(hh + 1) * t], NEG_INF)
        _softmax_first(s, vt[hh * HEAD_DIM:(hh + 1) * HEAD_DIM], m_ref, l_ref, acc_ref, hh)

    def step(j, s_slot):
        vt = vt_ref[j]
        for hh in range(2):
            _softmax_next(s_slot[hh * t:(hh + 1) * t, :], vt[hh * HEAD_DIM:(hh + 1) * HEAD_DIM],
                          m_ref, l_ref, acc_ref, hh, col_bias=bias_ref[hh, pl.ds(j, 1), :])

    _pipelined_kv_loop(0, qi, scores, s_a, s_b, step)

    o = jnp.concatenate([acc_ref[0] / l_ref[0], acc_ref[1] / l_ref[1]], axis=0)
    o_ref[...] = o.T.astype(BF16)


def _moba_attn(qk, vt, batch, seq):
    t = ATT_BLOCK
    nq = seq // t
    nb = seq // MOBA_BLOCK
    return pl.pallas_call(
        functools.partial(_moba_attn_kernel, nb=nb),
        grid=(batch, WIDTH_B // LANES, nq),
        in_specs=_att_specs(batch, _QB_BLK, _KB_BLK, _VB_BLK, seq),
        out_specs=pl.BlockSpec((t, LANES), lambda b, hp, qi: (b * nq + qi, hp)),
        out_shape=jax.ShapeDtypeStruct((batch * seq, WIDTH_B), BF16),
        scratch_shapes=[pltpu.VMEM((LANES, LANES), F32), pltpu.VMEM((2, nb, t), F32)] + _att_scratch(2),
        compiler_params=_cparams("parallel", "parallel", "arbitrary"),
        name="moba_attn",
    )(qk, qk, vt)


def _dilated_band_bias():
    c = LANES
    key = np.arange(2 * c)[:, None]
    qry = np.arange(c)[None, :]
    valid = np.where(key < c, qry <= key, key - c <= qry)
    no_prev = valid & (key >= c)
    return np.where(np.stack([valid, no_prev]), 0.0, NEG_INF).astype(np.float32)


def _dilated_unit(q_u, k_prev, k_own, vt_prev, vt_own, bias, lane):
    k_u = jnp.concatenate([k_prev, k_own], axis=0)
    n = k_u.shape[0]
    s_all = _nt_dot(_masked_stack(k_u, lane, HEAD_DIM, 2), q_u)
    vt_u = jnp.concatenate([vt_prev, vt_own], axis=1)
    outs, lses = [], []
    for hh in range(2):
        s = s_all[hh * n:(hh + 1) * n] + bias
        m = jnp.max(s, axis=0, keepdims=True)
        p = jnp.exp2(s - m)
        l = jnp.sum(p, axis=0, keepdims=True)
        acc = jnp.dot(vt_u[hh * HEAD_DIM:(hh + 1) * HEAD_DIM], p.astype(BF16), preferred_element_type=F32)
        outs.append(acc / l)
        lses.append(jnp.broadcast_to(m + jnp.log2(l), acc.shape))
    return jnp.concatenate(outs, axis=0).T, jnp.concatenate(lses, axis=0).T


def _dilated_attn_kernel(q1_ref, k1_ref, k1p_ref, vt1_ref, vt1p_ref,
                         q4_ref, k4_ref, k4p_ref, vt4_ref, vt4p_ref,
                         q16_ref, k16_ref, k16p_ref, v16_ref, v16p_ref, bias_ref, o_ref,
                         o1_ref, l1_ref, o4_ref, l4_ref, o16_ref, l16_ref):
    c = LANES
    lane = lax.broadcasted_iota(jnp.int32, (1, LANES), 1)
    first_tile = jnp.where(pl.program_id(2) == 0, 1, 0)
    bias = bias_ref[0]
    bias_prev_tile = bias_ref[first_tile]

    def rows(ref, chunk):
        return ref[pl.ds(pl.multiple_of(chunk * c, c), c), :]

    def vt_chunk(ref, blk, half):
        return ref[blk, :, half * c:(half + 1) * c]

    def unit1(g, u):
        ch = 4 * g + u
        if u == 0:
            k_prev, vt_prev = rows(k1_ref, ch - 1), vt_chunk(vt1_ref, 2 * g - 1, 1)
        else:
            k_prev, vt_prev = rows(k1_ref, ch - 1), vt_chunk(vt1_ref, 2 * g + (u - 1) // 2, (u - 1) % 2)
        o, lse = _dilated_unit(rows(q1_ref, ch), k_prev, rows(k1_ref, ch), vt_prev,
                               vt_chunk(vt1_ref, 2 * g + u // 2, u % 2), bias, lane)
        o1_ref[pl.ds(pl.multiple_of(ch * c, c), c), :] = o
        l1_ref[pl.ds(pl.multiple_of(ch * c, c), c), :] = lse

    def unit4(g, u):
        ch = 4 * g + u
        o, lse = _dilated_unit(rows(q4_ref, ch), rows(k4_ref, ch - 4), rows(k4_ref, ch),
                               vt_chunk(vt4_ref, 2 * (g - 1) + u // 2, u % 2),
                               vt_chunk(vt4_ref, 2 * g + u // 2, u % 2), bias, lane)
        o4_ref[pl.ds(g * 4 * c + u, c, stride=4), :] = o
        l4_ref[pl.ds(g * 4 * c + u, c, stride=4), :] = lse

    def unit16(g, u):
        ch = 4 * g + u
        vt_prev = rows(v16p_ref, ch).astype(F32).T.astype(BF16)
        vt_own = rows(v16_ref, ch).astype(F32).T.astype(BF16)
        o, lse = _dilated_unit(rows(q16_ref, ch), rows(k16p_ref, ch), rows(k16_ref, ch),
                               vt_prev, vt_own, bias_prev_tile, lane)
        o16_ref[pl.ds(ch, c, stride=16), :] = o
        l16_ref[pl.ds(ch, c, stride=16), :] = lse

    o, lse = _dilated_unit(q1_ref[:c, :], k1p_ref[...], k1_ref[:c, :], vt1p_ref[0, :, c:],
                           vt1_ref[0, :, :c], bias_prev_tile, lane)
    o1_ref[:c, :] = o
    l1_ref[:c, :] = lse
    for u in range(1, 4):
        unit1(0, u)
    for u in range(4):
        o, lse = _dilated_unit(q4_ref[u * c:(u + 1) * c, :], k4p_ref[u * c:(u + 1) * c, :],
                               k4_ref[u * c:(u + 1) * c, :], vt4p_ref[u // 2, :, (u % 2) * c:(u % 2 + 1) * c],
                               vt4_ref[u // 2, :, (u % 2) * c:(u % 2 + 1) * c], bias_prev_tile, lane)
        o4_ref[pl.ds(u, c, stride=4), :] = o
        l4_ref[pl.ds(u, c, stride=4), :] = lse

    def group(g, carry):
        for u in range(4):
            unit1(g, u)
            unit4(g, u)
        return carry

    lax.fori_loop(1, 4, group, 0)

    def group16(g, carry):
        for u in range(4):
            unit16(g, u)
        return carry

    lax.fori_loop(0, 4, group16, 0)

    def merge(i, carry):
        sl = pl.ds(pl.multiple_of(i * c, c), c)
        la, lb, lc = l1_ref[sl, :], l4_ref[sl, :], l16_ref[sl, :]
        top = jnp.maximum(jnp.maximum(la, lb), lc)
        wa, wb, wc = jnp.exp2(la - top), jnp.exp2(lb - top), jnp.exp2(lc - top)
        num = wa * o1_ref[sl, :] + wb * o4_ref[sl, :] + wc * o16_ref[sl, :]
        o_ref[sl, :] = (num / (wa + wb + wc)).astype(BF16)
        return carry

    lax.fori_loop(0, DIL_TILE // c, merge, 0)


def _dilated_attn(qk, vt, qk4, vt4, qkv16, bias, batch, seq):
    tile = DIL_TILE
    nt = seq // tile
    c = LANES
    t = ATT_BLOCK
    nhp = WIDTH_C // LANES
    row = lambda b, hp, ti: b * nt + ti

    def prev(blocks_per_tile):
        return lambda b, hp, ti: jnp.maximum((b * nt + ti) * blocks_per_tile - 1, 0)

    p16, p8, p4, p1 = prev(tile // c), prev(tile // t), prev(4), prev(1)
    full = (tile, LANES)
    vfull = (tile // t, LANES, t)
    in_specs = [
        pl.BlockSpec(full, lambda b, hp, ti: (row(b, hp, ti), _QC_BLK + hp)),
        pl.BlockSpec(full, lambda b, hp, ti: (row(b, hp, ti), _KC_BLK + hp)),
        pl.BlockSpec((c, LANES), lambda b, hp, ti: (p16(b, hp, ti), _KC_BLK + hp)),
        pl.BlockSpec(vfull, lambda b, hp, ti: (row(b, hp, ti), _VC_BLK + hp, 0)),
        pl.BlockSpec((1, LANES, t), lambda b, hp, ti: (p8(b, hp, ti), _VC_BLK + hp, 0)),
        pl.BlockSpec(full, lambda b, hp, ti: (row(b, hp, ti), hp)),
        pl.BlockSpec(full, lambda b, hp, ti: (row(b, hp, ti), nhp + hp)),
        pl.BlockSpec((4 * c, LANES), lambda b, hp, ti: (p4(b, hp, ti), nhp + hp)),
        pl.BlockSpec(vfull, lambda b, hp, ti: (row(b, hp, ti), hp, 0)),
        pl.BlockSpec((2, LANES, t), lambda b, hp, ti: (p4(b, hp, ti), hp, 0)),
        pl.BlockSpec(full, lambda b, hp, ti: (row(b, hp, ti), hp)),
        pl.BlockSpec(full, lambda b, hp, ti: (row(b, hp, ti), nhp + hp)),
        pl.BlockSpec(full, lambda b, hp, ti: (p1(b, hp, ti), nhp + hp)),
        pl.BlockSpec(full, lambda b, hp, ti: (row(b, hp, ti), 2 * nhp + hp)),
        pl.BlockSpec(full, lambda b, hp, ti: (p1(b, hp, ti), 2 * nhp + hp)),
        pl.BlockSpec((2, 2 * c, c), lambda b, hp, ti: (0, 0, 0)),
    ]
    return pl.pallas_call(
        _dilated_attn_kernel,
        grid=(batch, nhp, nt),
        in_specs=in_specs,
        out_specs=pl.BlockSpec(full, lambda b, hp, ti: (row(b, hp, ti), hp)),
        out_shape=jax.ShapeDtypeStruct((batch * seq, WIDTH_C), BF16),
        scratch_shapes=[pltpu.VMEM(full, F32)] * 6,
        compiler_params=_cparams("parallel", "parallel", "arbitrary"),
        name="dilated_attn",
    )(qk, qk, qk, vt, vt, qk4, qk4, qk4, vt4, vt4, qkv16, qkv16, qkv16, qkv16, qkv16, bias)


def _proj_out_kernel(oa_ref, ob_ref, oc_ref, w_ref, x_ref, g_ref, xo_ref, h_ref):
    acc = jnp.dot(oa_ref[...], w_ref[:WIDTH_A, :], preferred_element_type=F32)
    acc = acc + jnp.dot(ob_ref[...], w_ref[WIDTH_A:WIDTH_A + WIDTH_B, :], preferred_element_type=F32)
    acc = acc + jnp.dot(oc_ref[...], w_ref[WIDTH_A + WIDTH_B:, :], preferred_element_type=F32)
    xn = x_ref[...] + acc
    xo_ref[...] = xn
    ms = jnp.mean(xn * xn, axis=-1, keepdims=True)
    h_ref[...] = (xn * lax.rsqrt(ms + EPS) * g_ref[...]).astype(BF16)


def _proj_out(oa, ob, oc, w, x2d, g):
    m = x2d.shape[0]
    tm = ROW_TILE
    row = lambda i: (i, 0)
    fixed = lambda i: (0, 0)
    return pl.pallas_call(
        _proj_out_kernel,
        grid=(m // tm,),
        in_specs=[
            pl.BlockSpec((tm, WIDTH_A), row),
            pl.BlockSpec((tm, WIDTH_B), row),
            pl.BlockSpec((tm, WIDTH_C), row),
            pl.BlockSpec((D_MODEL, D_MODEL), fixed),
            pl.BlockSpec((tm, D_MODEL), row),
            pl.BlockSpec((1, D_MODEL), fixed),
        ],
        out_specs=[pl.BlockSpec((tm, D_MODEL), row), pl.BlockSpec((tm, D_MODEL), row)],
        out_shape=[jax.ShapeDtypeStruct((m, D_MODEL), F32), jax.ShapeDtypeStruct((m, D_MODEL), BF16)],
        compiler_params=_cparams("parallel"),
        name="proj_out",
    )(oa, ob, oc, w, x2d, g)


def _mlp_kernel(h_ref, halo_ref, x_ref, wup_ref, cw_ref, cb_ref, wd_ref, gf_ref, o_ref,
                hcat_ref, u_ref, *, tiles_per_seq, final_norm):
    i = pl.program_id(0)
    tm = h_ref.shape[0]
    nf = wd_ref.shape[0]
    seq_start = (i % tiles_per_seq) == 0
    halo = halo_ref[...]
    hcat_ref[:HALO, :] = jnp.where(seq_start, jnp.zeros_like(halo), halo)
    hcat_ref[HALO:, :] = h_ref[...]
    o_ref[...] = x_ref[...]

    def up(c, slot):
        for b in range(2):
            u_ref[slot, b] = jnp.dot(hcat_ref[...], wup_ref[b * nf + c], preferred_element_type=F32)

    def down(c, slot):
        def conv(b):
            cw = cw_ref[b * nf + c]
            return (cw[2:3] * u_ref[slot, b, HALO:HALO + tm, :]
                    + cw[1:2] * u_ref[slot, b, HALO - 1:HALO - 1 + tm, :]
                    + cw[0:1] * u_ref[slot, b, HALO - 2:HALO - 2 + tm, :]
                    + cb_ref[b * nf + c])

        gate = conv(0)
        val = conv(1)
        act = (gate * jax.nn.sigmoid(gate) * val).astype(BF16)
        o_ref[...] += jnp.dot(act, wd_ref[c], preferred_element_type=F32)

    assert nf % 2 == 1
    up(0, 0)

    def body(i, carry):
        c = 2 * i
        up(c + 1, 1)
        down(c, 0)
        up(c + 2, 0)
        down(c + 1, 1)
        return carry

    lax.fori_loop(0, nf // 2, body, 0)
    down(nf - 1, 0)

    if final_norm:
        y = o_ref[...]
        ms = jnp.mean(y * y, axis=-1, keepdims=True)
        o_ref[...] = y * lax.rsqrt(ms + EPS) * gf_ref[...]


def _mlp(h, x2d, w_up, conv_w, conv_b, w_down, g_final, seq, final_norm):
    m = x2d.shape[0]
    tm, tf = ROW_TILE, FF_TILE
    nf = D_FF // tf
    hb = tm // HALO
    wup3 = w_up.astype(BF16).reshape(D_MODEL, 2 * nf, tf).transpose(1, 0, 2)
    cw3 = conv_w.reshape(3, 2 * nf, tf).transpose(1, 0, 2)
    cb3 = conv_b.reshape(2 * nf, 1, tf)
    wd3 = w_down.astype(BF16).reshape(nf, tf, D_MODEL)
    row = lambda i: (i, 0)
    whole = lambda i: (0, 0, 0)
    once = pl.Buffered(1)
    return pl.pallas_call(
        functools.partial(_mlp_kernel, tiles_per_seq=seq // tm, final_norm=final_norm),
        grid=(m // tm,),
        in_specs=[
            pl.BlockSpec((tm, D_MODEL), row),
            pl.BlockSpec((HALO, D_MODEL), lambda i: (jnp.maximum(i * hb - 1, 0), 0)),
            pl.BlockSpec((tm, D_MODEL), row),
            pl.BlockSpec((2 * nf, D_MODEL, tf), whole, pipeline_mode=once),
            pl.BlockSpec((2 * nf, 3, tf), whole, pipeline_mode=once),
            pl.BlockSpec((2 * nf, 1, tf), whole, pipeline_mode=once),
            pl.BlockSpec((nf, tf, D_MODEL), whole, pipeline_mode=once),
            pl.BlockSpec((1, D_MODEL), lambda i: (0, 0)),
        ],
        out_specs=pl.BlockSpec((tm, D_MODEL), row),
        out_shape=jax.ShapeDtypeStruct((m, D_MODEL), F32),
        scratch_shapes=[
            pltpu.VMEM((HALO + tm, D_MODEL), BF16),
            pltpu.VMEM((2, 2, HALO + tm, tf), F32),
        ],
        compiler_params=_cparams("parallel"),
        name="conv_mlp",
    )(h, h, x2d, wup3, cw3, cb3, wd3, g_final)


def _split_w_in(w):
    a, b = WIDTH_A, WIDTH_B
    qk_cols = [w[:, 0:2 * a], w[:, 3 * a:3 * a + 2 * b], w[:, 3 * a + 3 * b:3 * a + 3 * b + 2 * WIDTH_C]]
    v_cols = [w[:, 2 * a:3 * a], w[:, 3 * a + 2 * b:3 * a + 3 * b], w[:, 3 * a + 3 * b + 2 * WIDTH_C:]]
    return (jnp.concatenate(qk_cols, axis=1).astype(BF16),
            jnp.concatenate(v_cols, axis=1).T.astype(BF16))


def kernel(x, positions, g_mix, w_in, w_out, lambda_q1, lambda_k1, lambda_q2, lambda_k2, g_diff, g_ffn,
           w_up, conv_w, conv_b, w_down, g_final):
    batch, seq, d_model = x.shape
    depth = g_mix.shape[0]
    assert d_model == D_MODEL and seq % DIL_TILE == 0 and seq % MOBA_BLOCK == 0
    m = batch * seq
    tab = _rope_tables(positions)
    bias = jnp.asarray(_dilated_band_bias())
    xf = x.reshape(m, d_model)
    for layer in range(depth):
        lam_init = 0.8 - 0.6 * math.exp(-0.3 * layer)
        wqk, wvt = _split_w_in(w_in[layer])
        qk, vt, qk4, vt4, qkv16 = _proj_in(xf, g_mix[layer][None, :], wqk, wvt, tab)
        g_t = jnp.broadcast_to(jnp.tile(g_diff[layer], 2)[:, None], (LANES, ATT_BLOCK))
        o_a = _diff_attn(qk, vt, lambda_q1[layer][None, :], lambda_k1[layer][None, :],
                         lambda_q2[layer][None, :], lambda_k2[layer][None, :], g_t, batch, seq, lam_init)
        o_b = _moba_attn(qk, vt, batch, seq)
        o_c = _dilated_attn(qk, vt, qk4, vt4, qkv16, bias, batch, seq)
        xf, h2 = _proj_out(o_a, o_b, o_c, w_out[layer].astype(BF16), xf, g_ffn[layer][None, :])
        xf = _mlp(h2, xf, w_up[layer].astype(BF16), conv_w[layer], conv_b[layer][None, :],
                  w_down[layer].astype(BF16), g_final[None, :], seq, layer == depth - 1)
    return xf.reshape(batch, seq, d_model)
```

```python
import functools
import math

import numpy as np
import jax
import jax.numpy as jnp
from jax import lax
from jax.experimental import pallas as pl
from jax.experimental.pallas import tpu as pltpu

F32 = jnp.float32
BF16 = jnp.bfloat16

D_MODEL = 1024
HEAD_DIM = 64
DIFF_DIM = 32
WIDTH_A = 256
WIDTH_B = 256
WIDTH_C = 512
ROPE_THETA = 500000.0
MOBA_BLOCK = 256
MOBA_TOPK = 3
DILATED_PATTERNS = ((128, 1), (512, 4), (2048, 16))
D_FF = 2816
EPS = 1e-6
NEG_INF = -1e30
LOG2E = math.log2(math.e)

LANES = 128
ATT_BLOCK = 256
DIL_TILE = 2048
ROW_TILE = 512
FF_TILE = 256
HALO = 16
ONES_ROWS = 16
VMEM_LIMIT = 48 * 1024 * 1024

_QK_SEGMENTS = (
    (0, 256, "qa", ("qk", 0)), (256, 256, "ka", ("ks", 0)),
    (512, 256, "q", ("qk", 256)), (768, 256, "k", ("ks", 1)),
    (1024, 512, "q", ("qk", 512)), (1536, 512, "k", ("qk", 1024)),
)
_W_QK_WIDTH = 2048
_QK_WIDTH = 1536
_V_WIDTH = 1024
_KS_STREAMS = (4, 2)
_QA_BLK, _QB_BLK, _QC_BLK, _KC_BLK = 0, 2, 4, 8
_VA_BLK, _VB_BLK, _VC_BLK = 0, 2, 4
_W_QC_START = 1024


def _cparams(*sem):
    return pltpu.CompilerParams(dimension_semantics=sem, vmem_limit_bytes=VMEM_LIMIT)


def _rope_table_kernel(pos_ref, tab_ref):
    pos = pos_ref[...].astype(F32)
    lane = lax.broadcasted_iota(jnp.int32, (1, 2 * LANES), 1)
    is_a = lane < LANES
    d = jnp.where(is_a, lane & (DIFF_DIM - 1), lane & (HEAD_DIM - 1))
    half = jnp.where(is_a, DIFF_DIM // 8, HEAD_DIM // 8)
    rot = 2 * half
    j = jnp.where(d < half, d, d - half)
    expo = -(j.astype(F32)) * 2.0 / rot.astype(F32)
    inv = jnp.where(d < rot, jnp.power(jnp.float32(ROPE_THETA), expo), 0.0)
    sign = jnp.where(d < half, -1.0, 1.0).astype(F32)
    ang = pos * inv
    tab_ref[:, : 2 * LANES] = jnp.cos(ang)
    tab_ref[:, 2 * LANES:] = jnp.sin(ang) * sign


def _rope_tables(positions):
    m = positions.size
    tr = ROW_TILE
    return pl.pallas_call(
        _rope_table_kernel,
        grid=(m // tr,),
        in_specs=[pl.BlockSpec((tr, 1), lambda i: (i, 0))],
        out_specs=pl.BlockSpec((tr, 4 * LANES), lambda i: (i, 0)),
        out_shape=jax.ShapeDtypeStruct((m, 4 * LANES), F32),
        compiler_params=_cparams("parallel"),
        name="rope_tables",
    )(positions.reshape(m, 1))


def _nt_dot(a, b):
    return lax.dot_general(a, b, (((1,), (1,)), ((), ())), preferred_element_type=F32)


def _proj_in_kernel(x_ref, g_ref, wqk_ref, wvt_ref, tab_ref, qk_ref, ksa_ref, ksb_ref, vt_ref, qk4_ref, vt4_ref,
                    qkv16_ref, cs_ref):
    tm = x_ref.shape[0]
    x = x_ref[...]
    ms = jnp.mean(x * x, axis=-1, keepdims=True)
    h = (x * lax.rsqrt(ms + EPS) * g_ref[...]).astype(BF16)
    lane = lax.broadcasted_iota(jnp.int32, (1, LANES), 1)
    ks_refs = (ksa_ref, ksb_ref)
    slab = 0
    for start, width, kind, dest in _QK_SEGMENTS:
        acc = jnp.dot(h, wqk_ref[:, start:start + width], preferred_element_type=F32)
        narrow = kind.endswith("a")
        dim = DIFF_DIM if narrow else HEAD_DIM
        half = dim // 8
        off = 0 if narrow else LANES
        cos_t = tab_ref[:, off:off + LANES]
        sin_t = tab_ref[:, 2 * LANES + off:3 * LANES + off]
        upper = (lane & (dim - 1)) >= half
        scale = dim ** -0.5 * LOG2E if kind.startswith("q") else None
        for c in range(width // LANES):
            xs = acc[:, c * LANES:(c + 1) * LANES]
            partner = jnp.where(upper, pltpu.roll(xs, half, 1), pltpu.roll(xs, LANES - half, 1))
            y = xs * cos_t + partner * sin_t
            if scale is not None:
                y = y * scale
            if dest[0] == "qk":
                col = dest[1] + c * LANES
                qk_ref[:, col:col + LANES] = y.astype(BF16)
            else:
                n = _KS_STREAMS[dest[1]]
                yb = y.astype(BF16)
                for i in range(n):
                    keep = (lane >= i * dim) & (lane < (i + 1) * dim)
                    col = (c * n + i) * LANES
                    ks_refs[dest[1]][:, col:col + LANES] = jnp.where(keep, yb, jnp.zeros_like(yb))
            if start >= _W_QC_START:
                cs_ref[slab] = y
                slab += 1
    t = ATT_BLOCK
    for r in range(_V_WIDTH // t):
        vt = _nt_dot(wvt_ref[r * t:(r + 1) * t, :], h)
        vtb = vt.astype(BF16)
        for c in range(vt_ref.shape[0]):
            vt_ref[c, r * t:(r + 1) * t, :] = vtb[:, c * t:(c + 1) * t]
        if r * t >= _VC_BLK * LANES:
            v_nat = vt.T
            for c in range(t // LANES):
                cs_ref[slab] = v_nat[:, c * LANES:(c + 1) * LANES]
                slab += 1
    n_qk = 2 * WIDTH_C // LANES
    n_all = 3 * WIDTH_C // LANES
    assert slab == n_all
    quarter = tm // 4
    for s in range(n_all):
        rows = [cs_ref[s, pl.ds(r, quarter, stride=4), :] for r in range(4)]
        if s < n_qk:
            for r in range(4):
                qk4_ref[r * quarter:(r + 1) * quarter, s * LANES:(s + 1) * LANES] = rows[r].astype(BF16)
        else:
            v4t = jnp.concatenate(rows, axis=0).T.astype(BF16)
            vs = s - n_qk
            for c in range(vt4_ref.shape[0]):
                vt4_ref[c, vs * LANES:(vs + 1) * LANES, :] = v4t[:, c * t:(c + 1) * t]
    per = tm // 16
    for s in range(n_all):
        for r in range(16):
            qkv16_ref[0, r, 0, :, s * LANES:(s + 1) * LANES] = (
                cs_ref[s, pl.ds(r, per, stride=16), :].astype(BF16))


def _proj_in(x2d, g, wqk, wvt, tab):
    m = x2d.shape[0]
    tm, t = ROW_TILE, ATT_BLOCK
    assert tm == DIL_TILE // 4 and tm == 4 * LANES
    per = tm // 16
    n_all = 3 * WIDTH_C // LANES
    outs = pl.pallas_call(
        _proj_in_kernel,
        grid=(m // tm,),
        in_specs=[
            pl.BlockSpec((tm, D_MODEL), lambda i: (i, 0)),
            pl.BlockSpec((1, D_MODEL), lambda i: (0, 0)),
            pl.BlockSpec((D_MODEL, _W_QK_WIDTH), lambda i: (0, 0)),
            pl.BlockSpec((_V_WIDTH, D_MODEL), lambda i: (0, 0)),
            pl.BlockSpec((tm, 4 * LANES), lambda i: (i, 0)),
        ],
        out_specs=[
            pl.BlockSpec((tm, _QK_WIDTH), lambda i: (i, 0)),
            pl.BlockSpec((tm, WIDTH_A * _KS_STREAMS[0]), lambda i: (i, 0)),
            pl.BlockSpec((tm, WIDTH_B * _KS_STREAMS[1]), lambda i: (i, 0)),
            pl.BlockSpec((tm // t, _V_WIDTH, t), lambda i: (i, 0, 0)),
            pl.BlockSpec((tm, 2 * WIDTH_C), lambda i: (i, 0)),
            pl.BlockSpec((tm // t, WIDTH_C, t), lambda i: (i, 0, 0)),
            pl.BlockSpec((1, 16, 1, per, 3 * WIDTH_C), lambda i: (i // 4, 0, i % 4, 0, 0)),
        ],
        out_shape=[
            jax.ShapeDtypeStruct((m, _QK_WIDTH), BF16),
            jax.ShapeDtypeStruct((m, WIDTH_A * _KS_STREAMS[0]), BF16),
            jax.ShapeDtypeStruct((m, WIDTH_B * _KS_STREAMS[1]), BF16),
            jax.ShapeDtypeStruct((m // t, _V_WIDTH, t), BF16),
            jax.ShapeDtypeStruct((m, 2 * WIDTH_C), BF16),
            jax.ShapeDtypeStruct((m // t, WIDTH_C, t), BF16),
            jax.ShapeDtypeStruct((m // DIL_TILE, 16, 4, per, 3 * WIDTH_C), BF16),
        ],
        scratch_shapes=[pltpu.VMEM((n_all, tm, LANES), F32)],
        compiler_params=_cparams("parallel"),
        name="proj_in",
    )(x2d, g, wqk, wvt, tab)
    qk, ksa, ksb, vt, qk4, vt4, qkv16 = outs
    return qk, ksa, ksb, vt, qk4, vt4, qkv16.reshape(m, 3 * WIDTH_C)


def _with_ones_rows(vt):
    return jnp.concatenate([vt, jnp.ones((ONES_ROWS, vt.shape[1]), vt.dtype)], axis=0)


def _softmax_first(ss, vts, m_ref, l_ref, acc_ref):
    ms = [jnp.max(s, axis=0, keepdims=True) for s in ss]
    ps = [jnp.exp2(s - m).astype(BF16) for s, m in zip(ss, ms)]
    for i, (p, m, vt) in enumerate(zip(ps, ms, vts)):
        r = jnp.dot(_with_ones_rows(vt), p, preferred_element_type=F32)
        m_ref[i] = m
        l_ref[i] = r[HEAD_DIM:HEAD_DIM + 1]
        acc_ref[i] = r[:HEAD_DIM]


def _softmax_next(ss, vts, m_ref, l_ref, acc_ref, col_biases=None):
    n = len(ss)
    m_prevs = [m_ref[i] for i in range(n)]
    m_blks = [jnp.max(s, axis=0, keepdims=True) for s in ss]
    if col_biases is not None:
        m_blks = [m + b for m, b in zip(m_blks, col_biases)]
    m_news = [jnp.maximum(a, b) for a, b in zip(m_prevs, m_blks)]
    shifts = m_news if col_biases is None else [m - b for m, b in zip(m_news, col_biases)]
    ps = [jnp.exp2(s - sh).astype(BF16) for s, sh in zip(ss, shifts)]
    for i in range(n):
        alpha = jnp.exp2(m_prevs[i] - m_news[i])
        r = jnp.dot(_with_ones_rows(vts[i]), ps[i], preferred_element_type=F32)
        m_ref[i] = m_news[i]
        l_ref[i] = alpha * l_ref[i] + r[HEAD_DIM:HEAD_DIM + 1]
        acc_ref[i] = alpha * acc_ref[i] + r[:HEAD_DIM]


def _causal_mask_t(t):
    key = lax.broadcasted_iota(jnp.int32, (t, t), 0)
    qry = lax.broadcasted_iota(jnp.int32, (t, t), 1)
    return key <= qry


def _masked_stack(k_blk, lane, width, n):
    zero = jnp.zeros_like(k_blk)
    return jnp.concatenate(
        [jnp.where((lane >= i * width) & (lane < (i + 1) * width), k_blk, zero) for i in range(n)], axis=0)


def _ks_block(ks_ref, j, n):
    blk = ks_ref[pl.ds(pl.multiple_of(j * ATT_BLOCK, ATT_BLOCK), ATT_BLOCK), :]
    return jnp.concatenate([blk[:, i * LANES:(i + 1) * LANES] for i in range(n)], axis=0)


def _att_scratch(n_streams):
    t = ATT_BLOCK
    return [
        pltpu.VMEM((n_streams, 1, t), F32),
        pltpu.VMEM((n_streams, 1, t), F32),
        pltpu.VMEM((n_streams, HEAD_DIM, t), F32),
        pltpu.VMEM((n_streams * t, t), F32),
        pltpu.VMEM((n_streams * t, t), F32),
    ]


def _pipelined_kv_loop(lo, hi, scores, s_a, s_b, step):
    n = hi - lo

    def body(i, carry):
        j = lo + 2 * i
        s_b[...] = scores(j + 1)
        step(j, s_a)
        s_a[...] = scores(j + 2)
        step(j + 1, s_b)
        return carry

    lax.fori_loop(0, n // 2, body, 0)

    @pl.when(n % 2 == 1)
    def _():
        step(hi - 1, s_a)


def _att_specs(q_blk, n_streams, v_blk, seq):
    t = ATT_BLOCK
    nq = seq // t
    return [
        pl.BlockSpec((t, LANES), lambda b, hp, qi: (b * nq + qi, q_blk + hp)),
        pl.BlockSpec((seq, n_streams * LANES), lambda b, hp, qi: (b, hp)),
        pl.BlockSpec((nq, LANES, t), lambda b, hp, qi: (b, v_blk + hp, 0)),
    ]


def _diff_attn_kernel(q_ref, k_ref, vt_ref, lq1_ref, lk1_ref, lq2_ref, lk2_ref, g_ref, o_ref,
                      m_ref, l_ref, acc_ref, s_a, s_b, *, lam_init):
    qi = pl.program_id(2)
    t = ATT_BLOCK
    q = q_ref[...]

    def scores(j):
        return _nt_dot(_ks_block(k_ref, j, 4), q)

    s_all = scores(qi)
    s_a[...] = scores(0)
    vt = vt_ref[qi]
    mask = _causal_mask_t(t)
    _softmax_first([jnp.where(mask, s_all[i * t:(i + 1) * t], NEG_INF) for i in range(4)],
                   [vt[(i // 2) * HEAD_DIM:(i // 2 + 1) * HEAD_DIM] for i in range(4)], m_ref, l_ref, acc_ref)

    def step(j, s_slot):
        vt = vt_ref[j]
        _softmax_next([s_slot[i * t:(i + 1) * t, :] for i in range(4)],
                      [vt[(i // 2) * HEAD_DIM:(i // 2 + 1) * HEAD_DIM] for i in range(4)], m_ref, l_ref, acc_ref)

    _pipelined_kv_loop(0, qi, scores, s_a, s_b, step)

    lam = (jnp.exp(jnp.sum(lq1_ref[...] * lk1_ref[...], axis=1, keepdims=True))
           - jnp.exp(jnp.sum(lq2_ref[...] * lk2_ref[...], axis=1, keepdims=True)) + lam_init)
    outs = []
    for hh in range(2):
        o = acc_ref[2 * hh] / l_ref[2 * hh] - lam * (acc_ref[2 * hh + 1] / l_ref[2 * hh + 1])
        ms = jnp.mean(o * o, axis=0, keepdims=True)
        outs.append(o * lax.rsqrt(ms + EPS))
    y = jnp.concatenate(outs, axis=0) * g_ref[...] * (1.0 - lam_init)
    o_ref[...] = y.T.astype(BF16)


def _diff_attn(qk, ksa, vt, lq1, lk1, lq2, lk2, g_t, batch, seq, lam_init):
    t = ATT_BLOCK
    nq = seq // t
    vec = pl.BlockSpec((1, DIFF_DIM), lambda b, hp, qi: (0, 0))
    return pl.pallas_call(
        functools.partial(_diff_attn_kernel, lam_init=lam_init),
        grid=(batch, WIDTH_A // LANES, nq),
        in_specs=_att_specs(_QA_BLK, _KS_STREAMS[0], _VA_BLK, seq) + [
            vec, vec, vec, vec,
            pl.BlockSpec((LANES, t), lambda b, hp, qi: (0, 0)),
        ],
        out_specs=pl.BlockSpec((t, LANES), lambda b, hp, qi: (b * nq + qi, hp)),
        out_shape=jax.ShapeDtypeStruct((batch * seq, WIDTH_A), BF16),
        scratch_shapes=_att_scratch(4),
        compiler_params=_cparams("parallel", "parallel", "arbitrary"),
        name="diff_attn",
    )(qk, ksa, vt, lq1, lk1, lq2, lk2, g_t)


def _moba_attn_kernel(q_ref, k_ref, vt_ref, o_ref, kmean_ref, bias_ref, m_ref, l_ref, acc_ref, s_a, s_b, *, nb):
    qi = pl.program_id(2)
    t = ATT_BLOCK

    @pl.when(qi == 0)
    def _():
        kmean_ref[...] = jnp.zeros_like(kmean_ref)
        for j in range(nb):
            kb = k_ref[j * t:(j + 1) * t, :].astype(F32)
            km = jnp.mean(kb, axis=0, keepdims=True)
            for hh in range(2):
                kmean_ref[hh * LANES + j:hh * LANES + j + 1, :] = km[:, hh * LANES:(hh + 1) * LANES]

    q = q_ref[...]
    gate_all = _nt_dot(kmean_ref[...].astype(BF16), q)
    blk = lax.broadcasted_iota(jnp.int32, (nb, t), 0)
    for hh in range(2):
        gate = gate_all[hh * LANES:hh * LANES + nb]
        cnt = jnp.zeros((nb, t), jnp.int32)
        for i in range(nb):
            gi = gate[i:i + 1, :]
            beats = (gi > gate) | ((gi == gate) & (blk > i))
            live = jnp.where(i < qi, 1, 0)
            cnt = cnt + jnp.where(beats, live, 0)
        sel = (cnt < MOBA_TOPK) & (blk < qi)
        bias_ref[hh] = jnp.where(sel, 0.0, NEG_INF)

    def scores(j):
        return _nt_dot(_ks_block(k_ref, j, 2), q)

    s_all = scores(qi)
    s_a[...] = scores(0)
    vt = vt_ref[qi]
    mask = _causal_mask_t(t)
    _softmax_first([jnp.where(mask, s_all[hh * t:(hh + 1) * t], NEG_INF) for hh in range(2)],
                   [vt[hh * HEAD_DIM:(hh + 1) * HEAD_DIM] for hh in range(2)], m_ref, l_ref, acc_ref)

    def step(j, s_slot):
        vt = vt_ref[j]
        _softmax_next([s_slot[hh * t:(hh + 1) * t, :] for hh in range(2)],
                      [vt[hh * HEAD_DIM:(hh + 1) * HEAD_DIM] for hh in range(2)], m_ref, l_ref, acc_ref,
                      col_biases=[bias_ref[hh, pl.ds(j, 1), :] for hh in range(2)])

    _pipelined_kv_loop(0, qi, scores, s_a, s_b, step)

    o = jnp.concatenate([acc_ref[0] / l_ref[0], acc_ref[1] / l_ref[1]], axis=0)
    o_ref[...] = o.T.astype(BF16)


def _moba_attn(qk, ksb, vt, batch, seq):
    t = ATT_BLOCK
    nq = seq // t
    nb = seq // MOBA_BLOCK
    return pl.pallas_call(
        functools.partial(_moba_attn_kernel, nb=nb),
        grid=(batch, WIDTH_B // LANES, nq),
        in_specs=_att_specs(_QB_BLK, _KS_STREAMS[1], _VB_BLK, seq),
        out_specs=pl.BlockSpec((t, LANES), lambda b, hp, qi: (b * nq + qi, hp)),
        out_shape=jax.ShapeDtypeStruct((batch * seq, WIDTH_B), BF16),
        scratch_shapes=[pltpu.VMEM((2 * LANES, LANES), F32), pltpu.VMEM((2, nb, t), F32)] + _att_scratch(2),
        compiler_params=_cparams("parallel", "parallel", "arbitrary"),
        name="moba_attn",
    )(qk, ksb, vt)


def _dilated_band_bias():
    c = LANES
    key = np.arange(2 * c)[:, None]
    qry = np.arange(c)[None, :]
    valid = np.where(key < c, qry <= key, key - c <= qry)
    no_prev = valid & (key >= c)
    return np.where(np.stack([valid, no_prev]), 0.0, NEG_INF).astype(np.float32)


def _dilated_units(units, lane):
    n = 2 * LANES
    s_alls = [_nt_dot(_masked_stack(jnp.concatenate([kp, ko], axis=0), lane, HEAD_DIM, 2), q)
              for q, kp, ko, _, _, _ in units]
    stats = []
    for s_all, (_, _, _, _, _, bias) in zip(s_alls, units):
        for hh in range(2):
            s = s_all[hh * n:(hh + 1) * n] + bias
            m = jnp.max(s, axis=0, keepdims=True)
            stats.append((m, jnp.exp2(s - m).astype(BF16)))
    results = []
    for i, (_, _, _, vtp, vto, _) in enumerate(units):
        vt_u = jnp.concatenate([vtp, vto], axis=1)
        outs, lses = [], []
        for hh in range(2):
            m, p = stats[2 * i + hh]
            r = jnp.dot(_with_ones_rows(vt_u[hh * HEAD_DIM:(hh + 1) * HEAD_DIM]), p, preferred_element_type=F32)
            acc, l = r[:HEAD_DIM], r[HEAD_DIM:HEAD_DIM + 1]
            outs.append(acc / l)
            lses.append(jnp.broadcast_to(m + jnp.log2(l), acc.shape))
        results.append((jnp.concatenate(outs, axis=0).T, jnp.concatenate(lses, axis=0).T))
    return results


def _dilated_attn_kernel(q1_ref, k1_ref, k1p_ref, vt1_ref, vt1p_ref,
                         q4_ref, k4_ref, k4p_ref, vt4_ref, vt4p_ref,
                         q16_ref, k16_ref, k16p_ref, v16_ref, v16p_ref, bias_ref, o_ref,
                         o1_ref, l1_ref, o4_ref, l4_ref, o16_ref, l16_ref):
    c = LANES
    lane = lax.broadcasted_iota(jnp.int32, (1, LANES), 1)
    first_tile = jnp.where(pl.program_id(2) == 0, 1, 0)
    bias = bias_ref[0]
    bias_prev_tile = bias_ref[first_tile]

    def rows(ref, chunk):
        return ref[pl.ds(pl.multiple_of(chunk * c, c), c), :]

    def vt_chunk(ref, blk, half):
        return ref[blk, :, half * c:(half + 1) * c]

    def unit1(g, u):
        ch = 4 * g + u
        vt_prev = vt_chunk(vt1_ref, 2 * g - 1, 1) if u == 0 else vt_chunk(vt1_ref, 2 * g + (u - 1) // 2, (u - 1) % 2)
        return (rows(q1_ref, ch), rows(k1_ref, ch - 1), rows(k1_ref, ch), vt_prev,
                vt_chunk(vt1_ref, 2 * g + u // 2, u % 2), bias)

    def store1(g, u, o, lse):
        sl = pl.ds(pl.multiple_of((4 * g + u) * c, c), c)
        o1_ref[sl, :] = o
        l1_ref[sl, :] = lse

    def unit4(g, u):
        ch = 4 * g + u
        return (rows(q4_ref, ch), rows(k4_ref, ch - 4), rows(k4_ref, ch),
                vt_chunk(vt4_ref, 2 * (g - 1) + u // 2, u % 2), vt_chunk(vt4_ref, 2 * g + u // 2, u % 2), bias)

    def store4(g, u, o, lse):
        o4_ref[pl.ds(g * 4 * c + u, c, stride=4), :] = o
        l4_ref[pl.ds(g * 4 * c + u, c, stride=4), :] = lse

    def unit16(g, u):
        ch = 4 * g + u
        return (rows(q16_ref, ch), rows(k16p_ref, ch), rows(k16_ref, ch),
                rows(v16p_ref, ch).astype(F32).T.astype(BF16), rows(v16_ref, ch).astype(F32).T.astype(BF16),
                bias_prev_tile)

    def store16(g, u, o, lse):
        o16_ref[pl.ds(4 * g + u, c, stride=16), :] = o
        l16_ref[pl.ds(4 * g + u, c, stride=16), :] = lse

    first = [(q1_ref[:c, :], k1p_ref[...], k1_ref[:c, :], vt1p_ref[0, :, c:], vt1_ref[0, :, :c], bias_prev_tile)]
    first += [unit1(0, u) for u in range(1, 4)]
    first += [(q4_ref[u * c:(u + 1) * c, :], k4p_ref[u * c:(u + 1) * c, :], k4_ref[u * c:(u + 1) * c, :],
               vt4p_ref[u // 2, :, (u % 2) * c:(u % 2 + 1) * c], vt4_ref[u // 2, :, (u % 2) * c:(u % 2 + 1) * c],
               bias_prev_tile) for u in range(4)]
    res = _dilated_units(first, lane)
    for u in range(4):
        store1(0, u, *res[u])
        store4(0, u, *res[4 + u])

    def group(g, carry):
        res = _dilated_units([unit1(g, u) for u in range(4)] + [unit4(g, u) for u in range(4)], lane)
        for u in range(4):
            store1(g, u, *res[u])
            store4(g, u, *res[4 + u])
        return carry

    lax.fori_loop(1, 4, group, 0)

    def group16(g, carry):
        res = _dilated_units([unit16(2 * g + h, u) for h in range(2) for u in range(4)], lane)
        for h in range(2):
            for u in range(4):
                store16(2 * g + h, u, *res[4 * h + u])
        return carry

    lax.fori_loop(0, 2, group16, 0)

    def merge(i, carry):
        sl = pl.ds(pl.multiple_of(i * c, c), c)
        la, lb, lc = l1_ref[sl, :], l4_ref[sl, :], l16_ref[sl, :]
        top = jnp.maximum(jnp.maximum(la, lb), lc)
        wa, wb, wc = jnp.exp2(la - top), jnp.exp2(lb - top), jnp.exp2(lc - top)
        num = wa * o1_ref[sl, :] + wb * o4_ref[sl, :] + wc * o16_ref[sl, :]
        o_ref[sl, :] = (num / (wa + wb + wc)).astype(BF16)
        return carry

    lax.fori_loop(0, DIL_TILE // c, merge, 0)


def _dilated_attn(qk, vt, qk4, vt4, qkv16, bias, batch, seq):
    tile = DIL_TILE
    nt = seq // tile
    c = LANES
    t = ATT_BLOCK
    nhp = WIDTH_C // LANES
    row = lambda b, hp, ti: b * nt + ti

    def prev(blocks_per_tile):
        return lambda b, hp, ti: jnp.maximum((b * nt + ti) * blocks_per_tile - 1, 0)

    p16, p8, p4, p1 = prev(tile // c), prev(tile // t), prev(4), prev(1)
    full = (tile, LANES)
    vfull = (tile // t, LANES, t)
    in_specs = [
        pl.BlockSpec(full, lambda b, hp, ti: (row(b, hp, ti), _QC_BLK + hp)),
        pl.BlockSpec(full, lambda b, hp, ti: (row(b, hp, ti), _KC_BLK + hp)),
        pl.BlockSpec((c, LANES), lambda b, hp, ti: (p16(b, hp, ti), _KC_BLK + hp)),
        pl.BlockSpec(vfull, lambda b, hp, ti: (row(b, hp, ti), _VC_BLK + hp, 0)),
        pl.BlockSpec((1, LANES, t), lambda b, hp, ti: (p8(b, hp, ti), _VC_BLK + hp, 0)),
        pl.BlockSpec(full, lambda b, hp, ti: (row(b, hp, ti), hp)),
        pl.BlockSpec(full, lambda b, hp, ti: (row(b, hp, ti), nhp + hp)),
        pl.BlockSpec((4 * c, LANES), lambda b, hp, ti: (p4(b, hp, ti), nhp + hp)),
        pl.BlockSpec(vfull, lambda b, hp, ti: (row(b, hp, ti), hp, 0)),
        pl.BlockSpec((2, LANES, t), lambda b, hp, ti: (p4(b, hp, ti), hp, 0)),
        pl.BlockSpec(full, lambda b, hp, ti: (row(b, hp, ti), hp)),
        pl.BlockSpec(full, lambda b, hp, ti: (row(b, hp, ti), nhp + hp)),
        pl.BlockSpec(full, lambda b, hp, ti: (p1(b, hp, ti), nhp + hp)),
        pl.BlockSpec(full, lambda b, hp, ti: (row(b, hp, ti), 2 * nhp + hp)),
        pl.BlockSpec(full, lambda b, hp, ti: (p1(b, hp, ti), 2 * nhp + hp)),
        pl.BlockSpec((2, 2 * c, c), lambda b, hp, ti: (0, 0, 0)),
    ]
    return pl.pallas_call(
        _dilated_attn_kernel,
        grid=(batch, nhp, nt),
        in_specs=in_specs,
        out_specs=pl.BlockSpec(full, lambda b, hp, ti: (row(b, hp, ti), hp)),
        out_shape=jax.ShapeDtypeStruct((batch * seq, WIDTH_C), BF16),
        scratch_shapes=[pltpu.VMEM(full, F32)] * 6,
        compiler_params=_cparams("parallel", "parallel", "arbitrary"),
        name="dilated_attn",
    )(qk, qk, qk, vt, vt, qk4, qk4, qk4, vt4, vt4, qkv16, qkv16, qkv16, qkv16, qkv16, bias)


def _proj_out_kernel(oa_ref, ob_ref, oc_ref, w_ref, x_ref, g_ref, xo_ref, h_ref):
    acc = jnp.dot(oa_ref[...], w_ref[:WIDTH_A, :], preferred_element_type=F32)
    acc = acc + jnp.dot(ob_ref[...], w_ref[WIDTH_A:WIDTH_A + WIDTH_B, :], preferred_element_type=F32)
    acc = acc + jnp.dot(oc_ref[...], w_ref[WIDTH_A + WIDTH_B:, :], preferred_element_type=F32)
    xn = x_ref[...] + acc
    xo_ref[...] = xn
    ms = jnp.mean(xn * xn, axis=-1, keepdims=True)
    h_ref[...] = (xn * lax.rsqrt(ms + EPS) * g_ref[...]).astype(BF16)


def _proj_out(oa, ob, oc, w, x2d, g):
    m = x2d.shape[0]
    tm = ROW_TILE
    row = lambda i: (i, 0)
    fixed = lambda i: (0, 0)
    return pl.pallas_call(
        _proj_out_kernel,
        grid=(m // tm,),
        in_specs=[
            pl.BlockSpec((tm, WIDTH_A), row),
            pl.BlockSpec((tm, WIDTH_B), row),
            pl.BlockSpec((tm, WIDTH_C), row),
            pl.BlockSpec((D_MODEL, D_MODEL), fixed),
            pl.BlockSpec((tm, D_MODEL), row),
            pl.BlockSpec((1, D_MODEL), fixed),
        ],
        out_specs=[pl.BlockSpec((tm, D_MODEL), row), pl.BlockSpec((tm, D_MODEL), row)],
        out_shape=[jax.ShapeDtypeStruct((m, D_MODEL), F32), jax.ShapeDtypeStruct((m, D_MODEL), BF16)],
        compiler_params=_cparams("parallel"),
        name="proj_out",
    )(oa, ob, oc, w, x2d, g)


def _mlp_kernel(h_ref, halo_ref, x_ref, wup_ref, cw_ref, cb_ref, wd_ref, gf_ref, o_ref,
                hcat_ref, u_ref, *, tiles_per_seq, final_norm):
    i = pl.program_id(0)
    tm = h_ref.shape[0]
    tf = FF_TILE
    nf = D_FF // tf
    seq_start = (i % tiles_per_seq) == 0
    halo = halo_ref[...]
    hcat_ref[:HALO, :] = jnp.where(seq_start, jnp.zeros_like(halo), halo)
    hcat_ref[HALO:, :] = h_ref[...]
    o_ref[...] = x_ref[...]

    def cols(b, c):
        return slice(b * D_FF + c * tf, b * D_FF + (c + 1) * tf)

    def up(c):
        for b in range(2):
            u_ref[c % 2, b] = jnp.dot(hcat_ref[...], wup_ref[:, cols(b, c)], preferred_element_type=F32)

    def down(c):
        def conv(b):
            cw = cw_ref[:, cols(b, c)]
            return (cw[2:3] * u_ref[c % 2, b, HALO:HALO + tm, :]
                    + cw[1:2] * u_ref[c % 2, b, HALO - 1:HALO - 1 + tm, :]
                    + cw[0:1] * u_ref[c % 2, b, HALO - 2:HALO - 2 + tm, :]
                    + cb_ref[:, cols(b, c)])

        gate = conv(0)
        val = conv(1)
        act = (gate * jax.nn.sigmoid(gate) * val).astype(BF16)
        o_ref[...] += jnp.dot(act, wd_ref[c * tf:(c + 1) * tf, :], preferred_element_type=F32)

    up(0)
    for c in range(nf):
        if c + 1 < nf:
            up(c + 1)
        down(c)

    if final_norm:
        y = o_ref[...]
        ms = jnp.mean(y * y, axis=-1, keepdims=True)
        o_ref[...] = y * lax.rsqrt(ms + EPS) * gf_ref[...]


def _mlp(h, x2d, w_up, conv_w, conv_b, w_down, g_final, seq, final_norm):
    m = x2d.shape[0]
    tm, tf = ROW_TILE, FF_TILE
    hb = tm // HALO
    row = lambda i: (i, 0)
    whole = lambda i: (0, 0)
    once = pl.Buffered(1)
    return pl.pallas_call(
        functools.partial(_mlp_kernel, tiles_per_seq=seq // tm, final_norm=final_norm),
        grid=(m // tm,),
        in_specs=[
            pl.BlockSpec((tm, D_MODEL), row),
            pl.BlockSpec((HALO, D_MODEL), lambda i: (jnp.maximum(i * hb - 1, 0), 0)),
            pl.BlockSpec((tm, D_MODEL), row),
            pl.BlockSpec((D_MODEL, 2 * D_FF), whole, pipeline_mode=once),
            pl.BlockSpec((3, 2 * D_FF), whole, pipeline_mode=once),
            pl.BlockSpec((1, 2 * D_FF), whole, pipeline_mode=once),
            pl.BlockSpec((D_FF, D_MODEL), whole, pipeline_mode=once),
            pl.BlockSpec((1, D_MODEL), whole),
        ],
        out_specs=pl.BlockSpec((tm, D_MODEL), row),
        out_shape=jax.ShapeDtypeStruct((m, D_MODEL), F32),
        scratch_shapes=[
            pltpu.VMEM((HALO + tm, D_MODEL), BF16),
            pltpu.VMEM((2, 2, HALO + tm, tf), F32),
        ],
        compiler_params=_cparams("parallel"),
        name="conv_mlp",
    )(h, h, x2d, w_up, conv_w, conv_b, w_down, g_final)


def _split_w_in(w):
    a, b = WIDTH_A, WIDTH_B
    qk_cols = [w[:, 0:2 * a], w[:, 3 * a:3 * a + 2 * b], w[:, 3 * a + 3 * b:3 * a + 3 * b + 2 * WIDTH_C]]
    v_cols = [w[:, 2 * a:3 * a], w[:, 3 * a + 2 * b:3 * a + 3 * b], w[:, 3 * a + 3 * b + 2 * WIDTH_C:]]
    return (jnp.concatenate(qk_cols, axis=1).astype(BF16),
            jnp.concatenate(v_cols, axis=1).T.astype(BF16))


def kernel(x, positions, g_mix, w_in, w_out, lambda_q1, lambda_k1, lambda_q2, lambda_k2, g_diff, g_ffn,
           w_up, conv_w, conv_b, w_down, g_final):
    batch, seq, d_model = x.shape
    depth = g_mix.shape[0]
    assert d_model == D_MODEL and seq % DIL_TILE == 0 and seq % MOBA_BLOCK == 0
    m = batch * seq
    tab = _rope_tables(positions)
    bias = jnp.asarray(_dilated_band_bias())
    xf = x.reshape(m, d_model)
    for layer in range(depth):
        lam_init = 0.8 - 0.6 * math.exp(-0.3 * layer)
        wqk, wvt = _split_w_in(w_in[layer])
        qk, ksa, ksb, vt, qk4, vt4, qkv16 = _proj_in(xf, g_mix[layer][None, :], wqk, wvt, tab)
        g_t = jnp.broadcast_to(jnp.tile(g_diff[layer], 2)[:, None], (LANES, ATT_BLOCK))
        o_a = _diff_attn(qk, ksa, vt, lambda_q1[layer][None, :], lambda_k1[layer][None, :],
                         lambda_q2[layer][None, :], lambda_k2[layer][None, :], g_t, batch, seq, lam_init)
        o_b = _moba_attn(qk, ksb, vt, batch, seq)
        o_c = _dilated_attn(qk, vt, qk4, vt4, qkv16, bias, batch, seq)
        xf, h2 = _proj_out(o_a, o_b, o_c, w_out[layer].astype(BF16), xf, g_ffn[layer][None, :])
        xf = _mlp(h2, xf, w_up[layer].astype(BF16), conv_w[layer], conv_b[layer][None, :],
                  w_down[layer].astype(BF16), g_final[None, :], seq, layer == depth - 1)
    return xf.reshape(batch, seq, d_model)
```

```python
import functools
import math

import numpy as np
import jax
import jax.numpy as jnp
from jax import lax
from jax.experimental import pallas as pl
from jax.experimental.pallas import tpu as pltpu

F32 = jnp.float32
BF16 = jnp.bfloat16

D_MODEL = 1024
HEAD_DIM = 64
DIFF_DIM = 32
WIDTH_A = 256
WIDTH_B = 256
WIDTH_C = 512
ROPE_THETA = 500000.0
MOBA_BLOCK = 256
MOBA_TOPK = 3
DILATED_PATTERNS = ((128, 1), (512, 4), (2048, 16))
D_FF = 2816
EPS = 1e-6
NEG_INF = -1e30
LOG2E = math.log2(math.e)

LANES = 128
ATT_BLOCK = 256
DIL_TILE = 2048
ROW_TILE = 512
FF_TILE = 256
HALO = 16
ONES_ROWS = 16
VMEM_LIMIT = 48 * 1024 * 1024

_QK_SEGMENTS = (
    (0, 256, "qa", ("qk", 0)), (256, 256, "ka", ("ks", 0)),
    (512, 256, "q", ("qk", 256)), (768, 256, "k", ("ks", 1)),
    (1024, 512, "q", ("qk", 512)), (1536, 512, "k", ("qk", 1024)),
)
_W_QK_WIDTH = 2048
_QK_WIDTH = 1536
_V_WIDTH = 1024
_KS_STREAMS = (4, 2)
_QA_BLK, _QB_BLK, _QC_BLK, _KC_BLK = 0, 2, 4, 8
_VA_BLK, _VB_BLK, _VC_BLK = 0, 2, 4
_W_QC_START = 1024


def _cparams(*sem):
    return pltpu.CompilerParams(dimension_semantics=sem, vmem_limit_bytes=VMEM_LIMIT)


def _rope_table_kernel(pos_ref, tab_ref):
    pos = pos_ref[...].astype(F32)
    tr = pos.shape[0]
    lane = lax.broadcasted_iota(jnp.int32, (1, LANES), 1)
    half, rot = HEAD_DIM // 8, HEAD_DIM // 4
    d = lane & (HEAD_DIM - 1)
    j = jnp.where(d < half, d, d - half)
    expo = -(j.astype(F32)) * 2.0 / rot
    inv = jnp.where(d < rot, jnp.power(jnp.float32(ROPE_THETA), expo), 0.0)
    ang = pos * inv
    cos_c, sin_c = jnp.cos(ang), jnp.sin(ang)
    da = lane & (DIFF_DIM - 1)
    half_a = DIFF_DIM // 8
    ja = jnp.where(da < half_a, da, da - half_a)
    src = jnp.where(da < 2 * half_a, half + 2 * ja, rot)
    idx = jnp.broadcast_to(src, (tr, LANES))
    cos_a = jnp.take_along_axis(cos_c, idx, axis=1)
    sin_a = jnp.take_along_axis(sin_c, idx, axis=1)
    tab_ref[:, :LANES] = cos_a
    tab_ref[:, LANES:2 * LANES] = cos_c
    tab_ref[:, 2 * LANES:3 * LANES] = sin_a * jnp.where(da < half_a, -1.0, 1.0).astype(F32)
    tab_ref[:, 3 * LANES:] = sin_c * jnp.where(d < half, -1.0, 1.0).astype(F32)


def _rope_tables(positions):
    m = positions.size
    tr = ROW_TILE
    return pl.pallas_call(
        _rope_table_kernel,
        grid=(m // tr,),
        in_specs=[pl.BlockSpec((tr, 1), lambda i: (i, 0))],
        out_specs=pl.BlockSpec((tr, 4 * LANES), lambda i: (i, 0)),
        out_shape=jax.ShapeDtypeStruct((m, 4 * LANES), F32),
        compiler_params=_cparams("parallel"),
        name="rope_tables",
    )(positions.reshape(m, 1))


def _nt_dot(a, b):
    return lax.dot_general(a, b, (((1,), (1,)), ((), ())), preferred_element_type=F32)


def _proj_in_kernel(x_ref, g_ref, wqk_ref, wvt_ref, tab_ref, qk_ref, ksa_ref, ksb_ref, vt_ref, qk4_ref, vt4_ref,
                    qkv16_ref, cs_ref):
    tm = x_ref.shape[0]
    x = x_ref[...]
    ms = jnp.mean(x * x, axis=-1, keepdims=True)
    h = (x * lax.rsqrt(ms + EPS) * g_ref[...]).astype(BF16)
    lane = lax.broadcasted_iota(jnp.int32, (1, LANES), 1)
    ks_refs = (ksa_ref, ksb_ref)
    slab = 0
    for start, width, kind, dest in _QK_SEGMENTS:
        acc = jnp.dot(h, wqk_ref[:, start:start + width], preferred_element_type=F32)
        narrow = kind.endswith("a")
        dim = DIFF_DIM if narrow else HEAD_DIM
        half = dim // 8
        off = 0 if narrow else LANES
        cos_t = tab_ref[:, off:off + LANES]
        sin_t = tab_ref[:, 2 * LANES + off:3 * LANES + off]
        upper = (lane & (dim - 1)) >= half
        scale = dim ** -0.5 * LOG2E if kind.startswith("q") else None
        for c in range(width // LANES):
            xs = acc[:, c * LANES:(c + 1) * LANES]
            partner = jnp.where(upper, pltpu.roll(xs, half, 1), pltpu.roll(xs, LANES - half, 1))
            y = xs * cos_t + partner * sin_t
            if scale is not None:
                y = y * scale
            if dest[0] == "qk":
                col = dest[1] + c * LANES
                qk_ref[:, col:col + LANES] = y.astype(BF16)
            else:
                n = _KS_STREAMS[dest[1]]
                yb = y.astype(BF16)
                for i in range(n):
                    keep = (lane >= i * dim) & (lane < (i + 1) * dim)
                    col = (c * n + i) * LANES
                    ks_refs[dest[1]][:, col:col + LANES] = jnp.where(keep, yb, jnp.zeros_like(yb))
            if start >= _W_QC_START:
                cs_ref[slab] = y
                slab += 1
    t = ATT_BLOCK
    for r in range(_V_WIDTH // t):
        vt = _nt_dot(wvt_ref[r * t:(r + 1) * t, :], h)
        vtb = vt.astype(BF16)
        for c in range(vt_ref.shape[0]):
            vt_ref[c, r * t:(r + 1) * t, :] = vtb[:, c * t:(c + 1) * t]
        if r * t >= _VC_BLK * LANES:
            v_nat = vt.T
            for c in range(t // LANES):
                cs_ref[slab] = v_nat[:, c * LANES:(c + 1) * LANES]
                slab += 1
    n_qk = 2 * WIDTH_C // LANES
    n_all = 3 * WIDTH_C // LANES
    assert slab == n_all
    quarter = tm // 4
    for s in range(n_all):
        rows = [cs_ref[s, pl.ds(r, quarter, stride=4), :] for r in range(4)]
        if s < n_qk:
            for r in range(4):
                qk4_ref[r * quarter:(r + 1) * quarter, s * LANES:(s + 1) * LANES] = rows[r].astype(BF16)
        else:
            v4t = jnp.concatenate(rows, axis=0).T.astype(BF16)
            vs = s - n_qk
            for c in range(vt4_ref.shape[0]):
                vt4_ref[c, vs * LANES:(vs + 1) * LANES, :] = v4t[:, c * t:(c + 1) * t]
    per = tm // 16
    for s in range(n_all):
        for r in range(16):
            qkv16_ref[0, r, 0, :, s * LANES:(s + 1) * LANES] = (
                cs_ref[s, pl.ds(r, per, stride=16), :].astype(BF16))


def _proj_in(x2d, g, wqk, wvt, tab):
    m = x2d.shape[0]
    tm, t = ROW_TILE, ATT_BLOCK
    assert tm == DIL_TILE // 4 and tm == 4 * LANES
    per = tm // 16
    n_all = 3 * WIDTH_C // LANES
    outs = pl.pallas_call(
        _proj_in_kernel,
        grid=(m // tm,),
        in_specs=[
            pl.BlockSpec((tm, D_MODEL), lambda i: (i, 0)),
            pl.BlockSpec((1, D_MODEL), lambda i: (0, 0)),
            pl.BlockSpec((D_MODEL, _W_QK_WIDTH), lambda i: (0, 0)),
            pl.BlockSpec((_V_WIDTH, D_MODEL), lambda i: (0, 0)),
            pl.BlockSpec((tm, 4 * LANES), lambda i: (i, 0)),
        ],
        out_specs=[
            pl.BlockSpec((tm, _QK_WIDTH), lambda i: (i, 0)),
            pl.BlockSpec((tm, WIDTH_A * _KS_STREAMS[0]), lambda i: (i, 0)),
            pl.BlockSpec((tm, WIDTH_B * _KS_STREAMS[1]), lambda i: (i, 0)),
            pl.BlockSpec((tm // t, _V_WIDTH, t), lambda i: (i, 0, 0)),
            pl.BlockSpec((tm, 2 * WIDTH_C), lambda i: (i, 0)),
            pl.BlockSpec((tm // t, WIDTH_C, t), lambda i: (i, 0, 0)),
            pl.BlockSpec((1, 16, 1, per, 3 * WIDTH_C), lambda i: (i // 4, 0, i % 4, 0, 0)),
        ],
        out_shape=[
            jax.ShapeDtypeStruct((m, _QK_WIDTH), BF16),
            jax.ShapeDtypeStruct((m, WIDTH_A * _KS_STREAMS[0]), BF16),
            jax.ShapeDtypeStruct((m, WIDTH_B * _KS_STREAMS[1]), BF16),
            jax.ShapeDtypeStruct((m // t, _V_WIDTH, t), BF16),
            jax.ShapeDtypeStruct((m, 2 * WIDTH_C), BF16),
            jax.ShapeDtypeStruct((m // t, WIDTH_C, t), BF16),
            jax.ShapeDtypeStruct((m // DIL_TILE, 16, 4, per, 3 * WIDTH_C), BF16),
        ],
        scratch_shapes=[pltpu.VMEM((n_all, tm, LANES), F32)],
        compiler_params=_cparams("parallel"),
        name="proj_in",
    )(x2d, g, wqk, wvt, tab)
    qk, ksa, ksb, vt, qk4, vt4, qkv16 = outs
    return qk, ksa, ksb, vt, qk4, vt4, qkv16.reshape(m, 3 * WIDTH_C)


def _with_ones_rows(vt):
    return jnp.concatenate([vt, jnp.ones((ONES_ROWS, vt.shape[1]), vt.dtype)], axis=0)


def _softmax_first(ss, vts, m_ref, l_ref, acc_ref):
    ms = [jnp.max(s, axis=0, keepdims=True) for s in ss]
    ps = [jnp.exp2(s - m).astype(BF16) for s, m in zip(ss, ms)]
    for i, (p, m, vt) in enumerate(zip(ps, ms, vts)):
        r = jnp.dot(_with_ones_rows(vt), p, preferred_element_type=F32)
        m_ref[i] = m
        l_ref[i] = r[HEAD_DIM:HEAD_DIM + 1]
        acc_ref[i] = r[:HEAD_DIM]


def _softmax_next(ss, vts, m_ref, l_ref, acc_ref, col_biases=None):
    n = len(ss)
    m_prevs = [m_ref[i] for i in range(n)]
    m_blks = [jnp.max(s, axis=0, keepdims=True) for s in ss]
    if col_biases is not None:
        m_blks = [m + b for m, b in zip(m_blks, col_biases)]
    m_news = [jnp.maximum(a, b) for a, b in zip(m_prevs, m_blks)]
    shifts = m_news if col_biases is None else [m - b for m, b in zip(m_news, col_biases)]
    ps = [jnp.exp2(s - sh).astype(BF16) for s, sh in zip(ss, shifts)]
    for i in range(n):
        alpha = jnp.exp2(m_prevs[i] - m_news[i])
        r = jnp.dot(_with_ones_rows(vts[i]), ps[i], preferred_element_type=F32)
        m_ref[i] = m_news[i]
        l_ref[i] = alpha * l_ref[i] + r[HEAD_DIM:HEAD_DIM + 1]
        acc_ref[i] = alpha * acc_ref[i] + r[:HEAD_DIM]


def _causal_mask_t(t):
    key = lax.broadcasted_iota(jnp.int32, (t, t), 0)
    qry = lax.broadcasted_iota(jnp.int32, (t, t), 1)
    return key <= qry


def _masked_stack(k_blk, lane, width, n):
    zero = jnp.zeros_like(k_blk)
    return jnp.concatenate(
        [jnp.where((lane >= i * width) & (lane < (i + 1) * width), k_blk, zero) for i in range(n)], axis=0)


def _ks_block(ks_ref, j, n):
    blk = ks_ref[pl.ds(pl.multiple_of(j * ATT_BLOCK, ATT_BLOCK), ATT_BLOCK), :]
    return jnp.concatenate([blk[:, i * LANES:(i + 1) * LANES] for i in range(n)], axis=0)


def _att_scratch(n_streams):
    t = ATT_BLOCK
    return [
        pltpu.VMEM((n_streams, 1, t), F32),
        pltpu.VMEM((n_streams, 1, t), F32),
        pltpu.VMEM((n_streams, HEAD_DIM, t), F32),
        pltpu.VMEM((n_streams * t, t), F32),
        pltpu.VMEM((n_streams * t, t), F32),
    ]


def _pipelined_kv_loop(lo, hi, scores, s_a, s_b, step):
    n = hi - lo

    def body(i, carry):
        j = lo + 2 * i
        s_b[...] = scores(j + 1)
        step(j, s_a)
        s_a[...] = scores(j + 2)
        step(j + 1, s_b)
        return carry

    lax.fori_loop(0, n // 2, body, 0)

    @pl.when(n % 2 == 1)
    def _():
        step(hi - 1, s_a)


def _att_specs(q_blk, n_streams, v_blk, seq):
    t = ATT_BLOCK
    nq = seq // t
    return [
        pl.BlockSpec((t, LANES), lambda b, hp, qi: (b * nq + qi, q_blk + hp)),
        pl.BlockSpec((seq, n_streams * LANES), lambda b, hp, qi: (b, hp)),
        pl.BlockSpec((nq, LANES, t), lambda b, hp, qi: (b, v_blk + hp, 0)),
    ]


def _diff_attn_kernel(q_ref, k_ref, vt_ref, lq1_ref, lk1_ref, lq2_ref, lk2_ref, g_ref, o_ref,
                      m_ref, l_ref, acc_ref, s_a, s_b, *, lam_init):
    qi = pl.program_id(2)
    t = ATT_BLOCK
    q = q_ref[...]

    def scores(j):
        return _nt_dot(_ks_block(k_ref, j, 4), q)

    s_all = scores(qi)
    s_a[...] = scores(0)
    vt = vt_ref[qi]
    mask = _causal_mask_t(t)
    _softmax_first([jnp.where(mask, s_all[i * t:(i + 1) * t], NEG_INF) for i in range(4)],
                   [vt[(i // 2) * HEAD_DIM:(i // 2 + 1) * HEAD_DIM] for i in range(4)], m_ref, l_ref, acc_ref)

    def step(j, s_slot):
        vt = vt_ref[j]
        _softmax_next([s_slot[i * t:(i + 1) * t, :] for i in range(4)],
                      [vt[(i // 2) * HEAD_DIM:(i // 2 + 1) * HEAD_DIM] for i in range(4)], m_ref, l_ref, acc_ref)

    _pipelined_kv_loop(0, qi, scores, s_a, s_b, step)

    lam = (jnp.exp(jnp.sum(lq1_ref[...] * lk1_ref[...], axis=1, keepdims=True))
           - jnp.exp(jnp.sum(lq2_ref[...] * lk2_ref[...], axis=1, keepdims=True)) + lam_init)
    outs = []
    for hh in range(2):
        o = acc_ref[2 * hh] / l_ref[2 * hh] - lam * (acc_ref[2 * hh + 1] / l_ref[2 * hh + 1])
        ms = jnp.mean(o * o, axis=0, keepdims=True)
        outs.append(o * lax.rsqrt(ms + EPS))
    y = jnp.concatenate(outs, axis=0) * g_ref[...] * (1.0 - lam_init)
    o_ref[...] = y.T.astype(BF16)


def _diff_attn(qk, ksa, vt, lq1, lk1, lq2, lk2, g_t, batch, seq, lam_init):
    t = ATT_BLOCK
    nq = seq // t
    vec = pl.BlockSpec((1, DIFF_DIM), lambda b, hp, qi: (0, 0))
    return pl.pallas_call(
        functools.partial(_diff_attn_kernel, lam_init=lam_init),
        grid=(batch, WIDTH_A // LANES, nq),
        in_specs=_att_specs(_QA_BLK, _KS_STREAMS[0], _VA_BLK, seq) + [
            vec, vec, vec, vec,
            pl.BlockSpec((LANES, t), lambda b, hp, qi: (0, 0)),
        ],
        out_specs=pl.BlockSpec((t, LANES), lambda b, hp, qi: (b * nq + qi, hp)),
        out_shape=jax.ShapeDtypeStruct((batch * seq, WIDTH_A), BF16),
        scratch_shapes=_att_scratch(4),
        compiler_params=_cparams("parallel", "parallel", "arbitrary"),
        name="diff_attn",
    )(qk, ksa, vt, lq1, lk1, lq2, lk2, g_t)


def _moba_attn_kernel(q_ref, k_ref, vt_ref, o_ref, kmean_ref, bias_ref, m_ref, l_ref, acc_ref, s_a, s_b, *, nb):
    qi = pl.program_id(1)
    t = ATT_BLOCK
    nh = WIDTH_B // HEAD_DIM

    @pl.when(qi == 0)
    def _():
        kmean_ref[...] = jnp.zeros_like(kmean_ref)
        for j in range(nb):
            kb = k_ref[j * t:(j + 1) * t, :].astype(F32)
            km = jnp.mean(kb, axis=0, keepdims=True)
            for i in range(nh):
                kmean_ref[i * LANES + j:i * LANES + j + 1, :] = km[:, i * LANES:(i + 1) * LANES]

    q = q_ref[...]
    qs = [q[:, hp * LANES:(hp + 1) * LANES] for hp in range(nh // 2)]

    def per_pair(stacked):
        rows = stacked.shape[0] // (nh // 2)
        return jnp.concatenate([_nt_dot(stacked[hp * rows:(hp + 1) * rows], qs[hp]) for hp in range(nh // 2)], axis=0)

    gate_all = per_pair(kmean_ref[...].astype(BF16))
    blk = lax.broadcasted_iota(jnp.int32, (nb, t), 0)
    for i in range(nh):
        gate = gate_all[i * LANES:i * LANES + nb]
        cnt = jnp.zeros((nb, t), jnp.int32)
        for r in range(nb):
            gr = gate[r:r + 1, :]
            beats = (gr > gate) | ((gr == gate) & (blk > r))
            live = jnp.where(r < qi, 1, 0)
            cnt = cnt + jnp.where(beats, live, 0)
        sel = (cnt < MOBA_TOPK) & (blk < qi)
        bias_ref[i] = jnp.where(sel, 0.0, NEG_INF)

    def scores(j):
        return per_pair(_ks_block(k_ref, j, nh))

    def values(j):
        vt = vt_ref[j]
        return [vt[i * HEAD_DIM:(i + 1) * HEAD_DIM] for i in range(nh)]

    s_all = scores(qi)
    s_a[...] = scores(0)
    mask = _causal_mask_t(t)
    _softmax_first([jnp.where(mask, s_all[i * t:(i + 1) * t], NEG_INF) for i in range(nh)], values(qi),
                   m_ref, l_ref, acc_ref)

    def step(j, s_slot):
        _softmax_next([s_slot[i * t:(i + 1) * t, :] for i in range(nh)], values(j), m_ref, l_ref, acc_ref,
                      col_biases=[bias_ref[i, pl.ds(j, 1), :] for i in range(nh)])

    _pipelined_kv_loop(0, qi, scores, s_a, s_b, step)

    o = jnp.concatenate([acc_ref[i] / l_ref[i] for i in range(nh)], axis=0)
    o_ref[...] = o.T.astype(BF16)


def _moba_attn(qk, ksb, vt, batch, seq):
    t = ATT_BLOCK
    nq = seq // t
    nb = seq // MOBA_BLOCK
    nh = WIDTH_B // HEAD_DIM
    assert _QB_BLK * LANES % WIDTH_B == 0 and _VB_BLK * LANES % WIDTH_B == 0
    return pl.pallas_call(
        functools.partial(_moba_attn_kernel, nb=nb),
        grid=(batch, nq),
        in_specs=[
            pl.BlockSpec((t, WIDTH_B), lambda b, qi: (b * nq + qi, _QB_BLK * LANES // WIDTH_B)),
            pl.BlockSpec((seq, nh * LANES), lambda b, qi: (b, 0)),
            pl.BlockSpec((nq, WIDTH_B, t), lambda b, qi: (b, _VB_BLK * LANES // WIDTH_B, 0)),
        ],
        out_specs=pl.BlockSpec((t, WIDTH_B), lambda b, qi: (b * nq + qi, 0)),
        out_shape=jax.ShapeDtypeStruct((batch * seq, WIDTH_B), BF16),
        scratch_shapes=[pltpu.VMEM((nh * LANES, LANES), F32), pltpu.VMEM((nh, nb, t), F32)] + _att_scratch(nh),
        compiler_params=_cparams("parallel", "arbitrary"),
        name="moba_attn",
    )(qk, ksb, vt)


def _dilated_band_bias():
    c = LANES
    key = np.arange(2 * c)[:, None]
    qry = np.arange(c)[None, :]
    valid = np.where(key < c, qry <= key, key - c <= qry)
    no_prev = valid & (key >= c)
    return np.where(np.stack([valid, no_prev]), 0.0, NEG_INF).astype(np.float32)


def _dilated_units(units, lane):
    n = 2 * LANES
    s_alls = [_nt_dot(_masked_stack(jnp.concatenate([kp, ko], axis=0), lane, HEAD_DIM, 2), q)
              for q, kp, ko, _, _, _ in units]
    stats = []
    for s_all, (_, _, _, _, _, bias) in zip(s_alls, units):
        for hh in range(2):
            s = s_all[hh * n:(hh + 1) * n] + bias
            m = jnp.max(s, axis=0, keepdims=True)
            stats.append((m, jnp.exp2(s - m).astype(BF16)))
    results = []
    for i, (_, _, _, vtp, vto, _) in enumerate(units):
        vt_u = jnp.concatenate([vtp, vto], axis=1)
        outs, lses = [], []
        for hh in range(2):
            m, p = stats[2 * i + hh]
            r = jnp.dot(_with_ones_rows(vt_u[hh * HEAD_DIM:(hh + 1) * HEAD_DIM]), p, preferred_element_type=F32)
            acc, l = r[:HEAD_DIM], r[HEAD_DIM:HEAD_DIM + 1]
            outs.append(acc / l)
            lses.append(jnp.broadcast_to(m + jnp.log2(l), acc.shape))
        results.append((jnp.concatenate(outs, axis=0).T, jnp.concatenate(lses, axis=0).T))
    return results


def _dilated_attn_kernel(q1_ref, k1_ref, k1p_ref, vt1_ref, vt1p_ref,
                         q4_ref, k4_ref, k4p_ref, vt4_ref, vt4p_ref,
                         q16_ref, k16_ref, k16p_ref, v16_ref, v16p_ref, bias_ref, o_ref,
                         o1_ref, l1_ref, o4_ref, l4_ref, o16_ref, l16_ref):
    c = LANES
    lane = lax.broadcasted_iota(jnp.int32, (1, LANES), 1)
    first_tile = jnp.where(pl.program_id(2) == 0, 1, 0)
    bias = bias_ref[0]
    bias_prev_tile = bias_ref[first_tile]

    def rows(ref, chunk):
        return ref[pl.ds(pl.multiple_of(chunk * c, c), c), :]

    def vt_chunk(ref, blk, half):
        return ref[blk, :, half * c:(half + 1) * c]

    def unit1(g, u):
        ch = 4 * g + u
        vt_prev = vt_chunk(vt1_ref, 2 * g - 1, 1) if u == 0 else vt_chunk(vt1_ref, 2 * g + (u - 1) // 2, (u - 1) % 2)
        return (rows(q1_ref, ch), rows(k1_ref, ch - 1), rows(k1_ref, ch), vt_prev,
                vt_chunk(vt1_ref, 2 * g + u // 2, u % 2), bias)

    def store1(g, u, o, lse):
        sl = pl.ds(pl.multiple_of((4 * g + u) * c, c), c)
        o1_ref[sl, :] = o
        l1_ref[sl, :] = lse

    def unit4(g, u):
        ch = 4 * g + u
        return (rows(q4_ref, ch), rows(k4_ref, ch - 4), rows(k4_ref, ch),
                vt_chunk(vt4_ref, 2 * (g - 1) + u // 2, u % 2), vt_chunk(vt4_ref, 2 * g + u // 2, u % 2), bias)

    def store4(g, u, o, lse):
        o4_ref[pl.ds(g * 4 * c + u, c, stride=4), :] = o
        l4_ref[pl.ds(g * 4 * c + u, c, stride=4), :] = lse

    def unit16(g, u):
        ch = 4 * g + u
        return (rows(q16_ref, ch), rows(k16p_ref, ch), rows(k16_ref, ch),
                rows(v16p_ref, ch).astype(F32).T.astype(BF16), rows(v16_ref, ch).astype(F32).T.astype(BF16),
                bias_prev_tile)

    def store16(g, u, o, lse):
        o16_ref[pl.ds(4 * g + u, c, stride=16), :] = o
        l16_ref[pl.ds(4 * g + u, c, stride=16), :] = lse

    first = [(q1_ref[:c, :], k1p_ref[...], k1_ref[:c, :], vt1p_ref[0, :, c:], vt1_ref[0, :, :c], bias_prev_tile)]
    first += [unit1(0, u) for u in range(1, 4)]
    first += [(q4_ref[u * c:(u + 1) * c, :], k4p_ref[u * c:(u + 1) * c, :], k4_ref[u * c:(u + 1) * c, :],
               vt4p_ref[u // 2, :, (u % 2) * c:(u % 2 + 1) * c], vt4_ref[u // 2, :, (u % 2) * c:(u % 2 + 1) * c],
               bias_prev_tile) for u in range(4)]
    res = _dilated_units(first, lane)
    for u in range(4):
        store1(0, u, *res[u])
        store4(0, u, *res[4 + u])

    def group(g, carry):
        res = _dilated_units([unit1(g, u) for u in range(4)] + [unit4(g, u) for u in range(4)], lane)
        for u in range(4):
            store1(g, u, *res[u])
            store4(g, u, *res[4 + u])
        return carry

    lax.fori_loop(1, 4, group, 0)

    def group16(g, carry):
        res = _dilated_units([unit16(2 * g + h, u) for h in range(2) for u in range(4)], lane)
        for h in range(2):
            for u in range(4):
                store16(2 * g + h, u, *res[4 * h + u])
        return carry

    lax.fori_loop(0, 2, group16, 0)

    def merge(i, carry):
        sl = pl.ds(pl.multiple_of(i * c, c), c)
        la, lb, lc = l1_ref[sl, :], l4_ref[sl, :], l16_ref[sl, :]
        top = jnp.maximum(jnp.maximum(la, lb), lc)
        wa, wb, wc = jnp.exp2(la - top), jnp.exp2(lb - top), jnp.exp2(lc - top)
        num = wa * o1_ref[sl, :] + wb * o4_ref[sl, :] + wc * o16_ref[sl, :]
        o_ref[sl, :] = (num / (wa + wb + wc)).astype(BF16)
        return carry

    lax.fori_loop(0, DIL_TILE // c, merge, 0)


def _dilated_attn(qk, vt, qk4, vt4, qkv16, bias, batch, seq):
    tile = DIL_TILE
    nt = seq // tile
    c = LANES
    t = ATT_BLOCK
    nhp = WIDTH_C // LANES
    row = lambda b, hp, ti: b * nt + ti

    def prev(blocks_per_tile):
        return lambda b, hp, ti: jnp.maximum((b * nt + ti) * blocks_per_tile - 1, 0)

    p16, p8, p4, p1 = prev(tile // c), prev(tile // t), prev(4), prev(1)
    full = (tile, LANES)
    vfull = (tile // t, LANES, t)
    in_specs = [
        pl.BlockSpec(full, lambda b, hp, ti: (row(b, hp, ti), _QC_BLK + hp)),
        pl.BlockSpec(full, lambda b, hp, ti: (row(b, hp, ti), _KC_BLK + hp)),
        pl.BlockSpec((c, LANES), lambda b, hp, ti: (p16(b, hp, ti), _KC_BLK + hp)),
        pl.BlockSpec(vfull, lambda b, hp, ti: (row(b, hp, ti), _VC_BLK + hp, 0)),
        pl.BlockSpec((1, LANES, t), lambda b, hp, ti: (p8(b, hp, ti), _VC_BLK + hp, 0)),
        pl.BlockSpec(full, lambda b, hp, ti: (row(b, hp, ti), hp)),
        pl.BlockSpec(full, lambda b, hp, ti: (row(b, hp, ti), nhp + hp)),
        pl.BlockSpec((4 * c, LANES), lambda b, hp, ti: (p4(b, hp, ti), nhp + hp)),
        pl.BlockSpec(vfull, lambda b, hp, ti: (row(b, hp, ti), hp, 0)),
        pl.BlockSpec((2, LANES, t), lambda b, hp, ti: (p4(b, hp, ti), hp, 0)),
        pl.BlockSpec(full, lambda b, hp, ti: (row(b, hp, ti), hp)),
        pl.BlockSpec(full, lambda b, hp, ti: (row(b, hp, ti), nhp + hp)),
        pl.BlockSpec(full, lambda b, hp, ti: (p1(b, hp, ti), nhp + hp)),
        pl.BlockSpec(full, lambda b, hp, ti: (row(b, hp, ti), 2 * nhp + hp)),
        pl.BlockSpec(full, lambda b, hp, ti: (p1(b, hp, ti), 2 * nhp + hp)),
        pl.BlockSpec((2, 2 * c, c), lambda b, hp, ti: (0, 0, 0)),
    ]
    return pl.pallas_call(
        _dilated_attn_kernel,
        grid=(batch, nhp, nt),
        in_specs=in_specs,
        out_specs=pl.BlockSpec(full, lambda b, hp, ti: (row(b, hp, ti), hp)),
        out_shape=jax.ShapeDtypeStruct((batch * seq, WIDTH_C), BF16),
        scratch_shapes=[pltpu.VMEM(full, F32)] * 6,
        compiler_params=_cparams("parallel", "parallel", "arbitrary"),
        name="dilated_attn",
    )(qk, qk, qk, vt, vt, qk4, qk4, qk4, vt4, vt4, qkv16, qkv16, qkv16, qkv16, qkv16, bias)


def _out_mlp_kernel(oa_ref, ob_ref, oc_ref, x_ref, oah_ref, obh_ref, och_ref, xh_ref, wo_ref, gn_ref,
                    wup_ref, cw_ref, cb_ref, wd_ref, gf_ref, o_ref, hcat_ref, u_ref, *, tiles_per_seq, final_norm):
    i = pl.program_id(0)
    tm = x_ref.shape[0]
    tf = FF_TILE
    nf = D_FF // tf
    seq_start = (i % tiles_per_seq) == 0

    def mix_norm(oa, ob, oc, x):
        acc = jnp.dot(oa, wo_ref[:WIDTH_A, :], preferred_element_type=F32)
        acc = acc + jnp.dot(ob, wo_ref[WIDTH_A:WIDTH_A + WIDTH_B, :], preferred_element_type=F32)
        acc = acc + jnp.dot(oc, wo_ref[WIDTH_A + WIDTH_B:, :], preferred_element_type=F32)
        xn = x + acc
        ms = jnp.mean(xn * xn, axis=-1, keepdims=True)
        return xn, (xn * lax.rsqrt(ms + EPS) * gn_ref[...]).astype(BF16)

    _, h_halo = mix_norm(oah_ref[...], obh_ref[...], och_ref[...], xh_ref[...])
    hcat_ref[:HALO, :] = jnp.where(seq_start, jnp.zeros_like(h_halo), h_halo)
    xn, h = mix_norm(oa_ref[...], ob_ref[...], oc_ref[...], x_ref[...])
    hcat_ref[HALO:, :] = h
    o_ref[...] = xn

    def cols(b, c):
        return slice(b * D_FF + c * tf, b * D_FF + (c + 1) * tf)

    def up(c):
        for b in range(2):
            u_ref[c % 2, b] = jnp.dot(hcat_ref[...], wup_ref[:, cols(b, c)], preferred_element_type=F32)

    def down(c):
        def conv(b):
            cw = cw_ref[:, cols(b, c)]
            return (cw[2:3] * u_ref[c % 2, b, HALO:HALO + tm, :]
                    + cw[1:2] * u_ref[c % 2, b, HALO - 1:HALO - 1 + tm, :]
                    + cw[0:1] * u_ref[c % 2, b, HALO - 2:HALO - 2 + tm, :]
                    + cb_ref[:, cols(b, c)])

        gate = conv(0)
        val = conv(1)
        act = (gate * jax.nn.sigmoid(gate) * val).astype(BF16)
        o_ref[...] += jnp.dot(act, wd_ref[c * tf:(c + 1) * tf, :], preferred_element_type=F32)

    up(0)
    for c in range(nf):
        if c + 1 < nf:
            up(c + 1)
        down(c)

    if final_norm:
        y = o_ref[...]
        ms = jnp.mean(y * y, axis=-1, keepdims=True)
        o_ref[...] = y * lax.rsqrt(ms + EPS) * gf_ref[...]


def _out_mlp(oa, ob, oc, x2d, w_out, g_ffn, w_up, conv_w, conv_b, w_down, g_final, seq, final_norm):
    m = x2d.shape[0]
    tm, tf = ROW_TILE, FF_TILE
    hb = tm // HALO
    row = lambda i: (i, 0)
    halo = lambda i: (jnp.maximum(i * hb - 1, 0), 0)
    whole = lambda i: (0, 0)
    once = pl.Buffered(1)
    return pl.pallas_call(
        functools.partial(_out_mlp_kernel, tiles_per_seq=seq // tm, final_norm=final_norm),
        grid=(m // tm,),
        in_specs=[
            pl.BlockSpec((tm, WIDTH_A), row),
            pl.BlockSpec((tm, WIDTH_B), row),
            pl.BlockSpec((tm, WIDTH_C), row),
            pl.BlockSpec((tm, D_MODEL), row),
            pl.BlockSpec((HALO, WIDTH_A), halo),
            pl.BlockSpec((HALO, WIDTH_B), halo),
            pl.BlockSpec((HALO, WIDTH_C), halo),
            pl.BlockSpec((HALO, D_MODEL), halo),
            pl.BlockSpec((D_MODEL, D_MODEL), whole, pipeline_mode=once),
            pl.BlockSpec((1, D_MODEL), whole),
            pl.BlockSpec((D_MODEL, 2 * D_FF), whole, pipeline_mode=once),
            pl.BlockSpec((3, 2 * D_FF), whole, pipeline_mode=once),
            pl.BlockSpec((1, 2 * D_FF), whole, pipeline_mode=once),
            pl.BlockSpec((D_FF, D_MODEL), whole, pipeline_mode=once),
            pl.BlockSpec((1, D_MODEL), whole),
        ],
        out_specs=pl.BlockSpec((tm, D_MODEL), row),
        out_shape=jax.ShapeDtypeStruct((m, D_MODEL), F32),
        scratch_shapes=[
            pltpu.VMEM((HALO + tm, D_MODEL), BF16),
            pltpu.VMEM((2, 2, HALO + tm, tf), F32),
        ],
        compiler_params=_cparams("parallel"),
        name="out_mlp",
    )(oa, ob, oc, x2d, oa, ob, oc, x2d, w_out, g_ffn, w_up, conv_w, conv_b, w_down, g_final)


def _split_w_in(w):
    a, b = WIDTH_A, WIDTH_B
    qk_cols = [w[:, 0:2 * a], w[:, 3 * a:3 * a + 2 * b], w[:, 3 * a + 3 * b:3 * a + 3 * b + 2 * WIDTH_C]]
    v_cols = [w[:, 2 * a:3 * a], w[:, 3 * a + 2 * b:3 * a + 3 * b], w[:, 3 * a + 3 * b + 2 * WIDTH_C:]]
    return (jnp.concatenate(qk_cols, axis=1).astype(BF16),
            jnp.concatenate(v_cols, axis=1).T.astype(BF16))


def kernel(x, positions, g_mix, w_in, w_out, lambda_q1, lambda_k1, lambda_q2, lambda_k2, g_diff, g_ffn,
           w_up, conv_w, conv_b, w_down, g_final):
    batch, seq, d_model = x.shape
    depth = g_mix.shape[0]
    assert d_model == D_MODEL and seq % DIL_TILE == 0 and seq % MOBA_BLOCK == 0
    m = batch * seq
    tab = _rope_tables(positions)
    bias = jnp.asarray(_dilated_band_bias())
    xf = x.reshape(m, d_model)
    for layer in range(depth):
        lam_init = 0.8 - 0.6 * math.exp(-0.3 * layer)
        wqk, wvt = _split_w_in(w_in[layer])
        qk, ksa, ksb, vt, qk4, vt4, qkv16 = _proj_in(xf, g_mix[layer][None, :], wqk, wvt, tab)
        g_t = jnp.broadcast_to(jnp.tile(g_diff[layer], 2)[:, None], (LANES, ATT_BLOCK))
        o_a = _diff_attn(qk, ksa, vt, lambda_q1[layer][None, :], lambda_k1[layer][None, :],
                         lambda_q2[layer][None, :], lambda_k2[layer][None, :], g_t, batch, seq, lam_init)
        o_b = _moba_attn(qk, ksb, vt, batch, seq)
        o_c = _dilated_attn(qk, vt, qk4, vt4, qkv16, bias, batch, seq)
        xf = _out_mlp(o_a, o_b, o_c, xf, w_out[layer].astype(BF16), g_ffn[layer][None, :],
                      w_up[layer].astype(BF16), conv_w[layer], conv_b[layer][None, :],
                      w_down[layer].astype(BF16), g_final[None, :], seq, layer == depth - 1)
    return xf.reshape(batch, seq, d_model)
```

```python
import functools
import math

import numpy as np
import jax
import jax.numpy as jnp
from jax import lax
from jax.experimental import pallas as pl
from jax.experimental.pallas import tpu as pltpu

F32 = jnp.float32
BF16 = jnp.bfloat16

D_MODEL = 1024
HEAD_DIM = 64
DIFF_DIM = 32
WIDTH_A = 256
WIDTH_B = 256
WIDTH_C = 512
ROPE_THETA = 500000.0
MOBA_BLOCK = 256
MOBA_TOPK = 3
DILATED_PATTERNS = ((128, 1), (512, 4), (2048, 16))
D_FF = 2816
EPS = 1e-6
NEG_INF = -1e30
LOG2E = math.log2(math.e)

LANES = 128
ATT_BLOCK = 256
DIL_TILE = 2048
ROW_TILE = 512
FF_TILE = 256
HALO = 16
ONES_ROWS = 16
VMEM_LIMIT = 48 * 1024 * 1024

_QK_SEGMENTS = (
    (0, 256, "qa", ("qk", 0)), (256, 256, "ka", ("ks", 0)),
    (512, 256, "q", ("qk", 256)), (768, 256, "k", ("ks", 1)),
    (1024, 512, "q", ("qk", 512)), (1536, 512, "k", ("qk", 1024)),
)
_W_QK_WIDTH = 2048
_QK_WIDTH = 1536
_V_WIDTH = 1024
_KS_STREAMS = (4, 2)
_QA_BLK, _QB_BLK, _QC_BLK, _KC_BLK = 0, 2, 4, 8
_VA_BLK, _VB_BLK, _VC_BLK = 0, 2, 4
_W_QC_START = 1024


def _cparams(*sem):
    return pltpu.CompilerParams(dimension_semantics=sem, vmem_limit_bytes=VMEM_LIMIT)


def _rope_table_kernel(pos_ref, tab_ref):
    pos = pos_ref[...].astype(F32)
    tr = pos.shape[0]
    lane = lax.broadcasted_iota(jnp.int32, (1, LANES), 1)
    half, rot = HEAD_DIM // 8, HEAD_DIM // 4
    d = lane & (HEAD_DIM - 1)
    j = jnp.where(d < half, d, d - half)
    expo = -(j.astype(F32)) * 2.0 / rot
    inv = jnp.where(d < rot, jnp.power(jnp.float32(ROPE_THETA), expo), 0.0)
    ang = pos * inv
    cos_c, sin_c = jnp.cos(ang), jnp.sin(ang)
    da = lane & (DIFF_DIM - 1)
    half_a = DIFF_DIM // 8
    ja = jnp.where(da < half_a, da, da - half_a)
    src = jnp.where(da < 2 * half_a, half + 2 * ja, rot)
    idx = jnp.broadcast_to(src, (tr, LANES))
    cos_a = jnp.take_along_axis(cos_c, idx, axis=1)
    sin_a = jnp.take_along_axis(sin_c, idx, axis=1)
    tab_ref[:, :LANES] = cos_a
    tab_ref[:, LANES:2 * LANES] = cos_c
    tab_ref[:, 2 * LANES:3 * LANES] = sin_a * jnp.where(da < half_a, -1.0, 1.0).astype(F32)
    tab_ref[:, 3 * LANES:] = sin_c * jnp.where(d < half, -1.0, 1.0).astype(F32)


def _rope_tables(positions):
    m = positions.size
    tr = ROW_TILE
    return pl.pallas_call(
        _rope_table_kernel,
        grid=(m // tr,),
        in_specs=[pl.BlockSpec((tr, 1), lambda i: (i, 0))],
        out_specs=pl.BlockSpec((tr, 4 * LANES), lambda i: (i, 0)),
        out_shape=jax.ShapeDtypeStruct((m, 4 * LANES), F32),
        compiler_params=_cparams("parallel"),
        name="rope_tables",
    )(positions.reshape(m, 1))


def _nt_dot(a, b):
    return lax.dot_general(a, b, (((1,), (1,)), ((), ())), preferred_element_type=F32)


def _proj_in_kernel(x_ref, g_ref, wqk_ref, wvt_ref, tab_ref, qk_ref, ksa_ref, ksb_ref, vt_ref, qk4_ref, vt4_ref,
                    qkv16_ref, cs_ref):
    tm = x_ref.shape[0]
    x = x_ref[...]
    ms = jnp.mean(x * x, axis=-1, keepdims=True)
    h = (x * lax.rsqrt(ms + EPS) * g_ref[...]).astype(BF16)
    lane = lax.broadcasted_iota(jnp.int32, (1, LANES), 1)
    ks_refs = (ksa_ref, ksb_ref)
    slab = 0
    for start, width, kind, dest in _QK_SEGMENTS:
        acc = jnp.dot(h, wqk_ref[:, start:start + width], preferred_element_type=F32)
        narrow = kind.endswith("a")
        dim = DIFF_DIM if narrow else HEAD_DIM
        half = dim // 8
        off = 0 if narrow else LANES
        cos_t = tab_ref[:, off:off + LANES]
        sin_t = tab_ref[:, 2 * LANES + off:3 * LANES + off]
        upper = (lane & (dim - 1)) >= half
        scale = dim ** -0.5 * LOG2E if kind.startswith("q") else None
        for c in range(width // LANES):
            xs = acc[:, c * LANES:(c + 1) * LANES]
            partner = jnp.where(upper, pltpu.roll(xs, half, 1), pltpu.roll(xs, LANES - half, 1))
            y = xs * cos_t + partner * sin_t
            if scale is not None:
                y = y * scale
            if dest[0] == "qk":
                col = dest[1] + c * LANES
                qk_ref[:, col:col + LANES] = y.astype(BF16)
            else:
                n = _KS_STREAMS[dest[1]]
                yb = y.astype(BF16)
                for i in range(n):
                    keep = (lane >= i * dim) & (lane < (i + 1) * dim)
                    col = (c * n + i) * LANES
                    ks_refs[dest[1]][:, col:col + LANES] = jnp.where(keep, yb, jnp.zeros_like(yb))
            if start >= _W_QC_START:
                cs_ref[slab] = y
                slab += 1
    t = ATT_BLOCK
    for r in range(_V_WIDTH // t):
        vt = _nt_dot(wvt_ref[r * t:(r + 1) * t, :], h)
        vtb = vt.astype(BF16)
        for c in range(vt_ref.shape[0]):
            vt_ref[c, r * t:(r + 1) * t, :] = vtb[:, c * t:(c + 1) * t]
        if r * t >= _VC_BLK * LANES:
            v_nat = vt.T
            for c in range(t // LANES):
                cs_ref[slab] = v_nat[:, c * LANES:(c + 1) * LANES]
                slab += 1
    n_qk = 2 * WIDTH_C // LANES
    n_all = 3 * WIDTH_C // LANES
    assert slab == n_all
    quarter = tm // 4
    for s in range(n_all):
        rows = [cs_ref[s, pl.ds(r, quarter, stride=4), :] for r in range(4)]
        if s < n_qk:
            for r in range(4):
                qk4_ref[r * quarter:(r + 1) * quarter, s * LANES:(s + 1) * LANES] = rows[r].astype(BF16)
        else:
            v4t = jnp.concatenate(rows, axis=0).T.astype(BF16)
            vs = s - n_qk
            for c in range(vt4_ref.shape[0]):
                vt4_ref[c, vs * LANES:(vs + 1) * LANES, :] = v4t[:, c * t:(c + 1) * t]
    per = tm // 16
    for s in range(n_all):
        for r in range(16):
            qkv16_ref[0, r, 0, :, s * LANES:(s + 1) * LANES] = (
                cs_ref[s, pl.ds(r, per, stride=16), :].astype(BF16))


def _proj_in(x2d, g, wqk, wvt, tab):
    m = x2d.shape[0]
    tm, t = ROW_TILE, ATT_BLOCK
    assert tm == DIL_TILE // 4 and tm == 4 * LANES
    per = tm // 16
    n_all = 3 * WIDTH_C // LANES
    outs = pl.pallas_call(
        _proj_in_kernel,
        grid=(m // tm,),
        in_specs=[
            pl.BlockSpec((tm, D_MODEL), lambda i: (i, 0)),
            pl.BlockSpec((1, D_MODEL), lambda i: (0, 0)),
            pl.BlockSpec((D_MODEL, _W_QK_WIDTH), lambda i: (0, 0)),
            pl.BlockSpec((_V_WIDTH, D_MODEL), lambda i: (0, 0)),
            pl.BlockSpec((tm, 4 * LANES), lambda i: (i, 0)),
        ],
        out_specs=[
            pl.BlockSpec((tm, _QK_WIDTH), lambda i: (i, 0)),
            pl.BlockSpec((tm, WIDTH_A * _KS_STREAMS[0]), lambda i: (i, 0)),
            pl.BlockSpec((tm, WIDTH_B * _KS_STREAMS[1]), lambda i: (i, 0)),
            pl.BlockSpec((tm // t, _V_WIDTH, t), lambda i: (i, 0, 0)),
            pl.BlockSpec((tm, 2 * WIDTH_C), lambda i: (i, 0)),
            pl.BlockSpec((tm // t, WIDTH_C, t), lambda i: (i, 0, 0)),
            pl.BlockSpec((1, 16, 1, per, 3 * WIDTH_C), lambda i: (i // 4, 0, i % 4, 0, 0)),
        ],
        out_shape=[
            jax.ShapeDtypeStruct((m, _QK_WIDTH), BF16),
            jax.ShapeDtypeStruct((m, WIDTH_A * _KS_STREAMS[0]), BF16),
            jax.ShapeDtypeStruct((m, WIDTH_B * _KS_STREAMS[1]), BF16),
            jax.ShapeDtypeStruct((m // t, _V_WIDTH, t), BF16),
            jax.ShapeDtypeStruct((m, 2 * WIDTH_C), BF16),
            jax.ShapeDtypeStruct((m // t, WIDTH_C, t), BF16),
            jax.ShapeDtypeStruct((m // DIL_TILE, 16, 4, per, 3 * WIDTH_C), BF16),
        ],
        scratch_shapes=[pltpu.VMEM((n_all, tm, LANES), F32)],
        compiler_params=_cparams("parallel"),
        name="proj_in",
    )(x2d, g, wqk, wvt, tab)
    qk, ksa, ksb, vt, qk4, vt4, qkv16 = outs
    return qk, ksa, ksb, vt, qk4, vt4, qkv16.reshape(m, 3 * WIDTH_C)


def _with_ones_rows(vt):
    return jnp.concatenate([vt, jnp.ones((ONES_ROWS, vt.shape[1]), vt.dtype)], axis=0)


def _softmax_first(ss, vts, m_ref, l_ref, acc_ref):
    ms = [jnp.max(s, axis=0, keepdims=True) for s in ss]
    ps = [jnp.exp2(s - m).astype(BF16) for s, m in zip(ss, ms)]
    for i, (p, m, vt) in enumerate(zip(ps, ms, vts)):
        r = jnp.dot(_with_ones_rows(vt), p, preferred_element_type=F32)
        m_ref[i] = m
        l_ref[i] = r[HEAD_DIM:HEAD_DIM + 1]
        acc_ref[i] = r[:HEAD_DIM]


def _softmax_next(ss, vts, m_ref, l_ref, acc_ref, col_biases=None):
    n = len(ss)
    m_prevs = [m_ref[i] for i in range(n)]
    m_blks = [jnp.max(s, axis=0, keepdims=True) for s in ss]
    if col_biases is not None:
        m_blks = [m + b for m, b in zip(m_blks, col_biases)]
    m_news = [jnp.maximum(a, b) for a, b in zip(m_prevs, m_blks)]
    shifts = m_news if col_biases is None else [m - b for m, b in zip(m_news, col_biases)]
    ps = [jnp.exp2(s - sh).astype(BF16) for s, sh in zip(ss, shifts)]
    for i in range(n):
        alpha = jnp.exp2(m_prevs[i] - m_news[i])
        r = jnp.dot(_with_ones_rows(vts[i]), ps[i], preferred_element_type=F32)
        m_ref[i] = m_news[i]
        l_ref[i] = alpha * l_ref[i] + r[HEAD_DIM:HEAD_DIM + 1]
        acc_ref[i] = alpha * acc_ref[i] + r[:HEAD_DIM]


def _causal_mask_t(t):
    key = lax.broadcasted_iota(jnp.int32, (t, t), 0)
    qry = lax.broadcasted_iota(jnp.int32, (t, t), 1)
    return key <= qry


def _masked_stack(k_blk, lane, width, n):
    zero = jnp.zeros_like(k_blk)
    return jnp.concatenate(
        [jnp.where((lane >= i * width) & (lane < (i + 1) * width), k_blk, zero) for i in range(n)], axis=0)


def _ks_block(ks_ref, j, n):
    blk = ks_ref[pl.ds(pl.multiple_of(j * ATT_BLOCK, ATT_BLOCK), ATT_BLOCK), :]
    return jnp.concatenate([blk[:, i * LANES:(i + 1) * LANES] for i in range(n)], axis=0)


def _att_scratch(n_streams):
    t = ATT_BLOCK
    return [
        pltpu.VMEM((n_streams, 1, t), F32),
        pltpu.VMEM((n_streams, 1, t), F32),
        pltpu.VMEM((n_streams, HEAD_DIM, t), F32),
        pltpu.VMEM((n_streams * t, t), F32),
        pltpu.VMEM((n_streams * t, t), F32),
    ]


def _pipelined_kv_loop(lo, hi, scores, s_a, s_b, step):
    n = hi - lo

    def body(i, carry):
        j = lo + 2 * i
        s_b[...] = scores(j + 1)
        step(j, s_a)
        s_a[...] = scores(j + 2)
        step(j + 1, s_b)
        return carry

    lax.fori_loop(0, n // 2, body, 0)

    @pl.when(n % 2 == 1)
    def _():
        step(hi - 1, s_a)


def _group_specs(q_col, width, n_copies, vt_row, seq):
    t = ATT_BLOCK
    nq = seq // t
    assert q_col % width == 0 and vt_row % width == 0
    return [
        pl.BlockSpec((t, width), lambda b, qi: (b * nq + qi, q_col // width)),
        pl.BlockSpec((seq, n_copies * LANES), lambda b, qi: (b, 0)),
        pl.BlockSpec((nq, width, t), lambda b, qi: (b, vt_row // width, 0)),
    ]


def _pair_scores(stacked, qs):
    rows = stacked.shape[0] // len(qs)
    return jnp.concatenate([_nt_dot(stacked[hp * rows:(hp + 1) * rows], qs[hp]) for hp in range(len(qs))], axis=0)


def _diff_attn_kernel(q_ref, k_ref, vt_ref, lq1_ref, lk1_ref, lq2_ref, lk2_ref, g_ref, o_ref,
                      m_ref, l_ref, acc_ref, s_a, s_b, *, lam_init):
    qi = pl.program_id(1)
    t = ATT_BLOCK
    nh = WIDTH_A // HEAD_DIM
    ns = 2 * nh
    q = q_ref[...]
    qs = [q[:, hp * LANES:(hp + 1) * LANES] for hp in range(WIDTH_A // LANES)]

    def scores(j):
        return _pair_scores(_ks_block(k_ref, j, ns), qs)

    def values(j):
        vt = vt_ref[j]
        return [vt[(i // 2) * HEAD_DIM:(i // 2 + 1) * HEAD_DIM] for i in range(ns)]

    s_all = scores(qi)
    s_a[...] = scores(0)
    mask = _causal_mask_t(t)
    _softmax_first([jnp.where(mask, s_all[i * t:(i + 1) * t], NEG_INF) for i in range(ns)], values(qi),
                   m_ref, l_ref, acc_ref)

    def step(j, s_slot):
        _softmax_next([s_slot[i * t:(i + 1) * t, :] for i in range(ns)], values(j), m_ref, l_ref, acc_ref)

    _pipelined_kv_loop(0, qi, scores, s_a, s_b, step)

    lam = (jnp.exp(jnp.sum(lq1_ref[...] * lk1_ref[...], axis=1, keepdims=True))
           - jnp.exp(jnp.sum(lq2_ref[...] * lk2_ref[...], axis=1, keepdims=True)) + lam_init)
    outs = []
    for hh in range(nh):
        o = acc_ref[2 * hh] / l_ref[2 * hh] - lam * (acc_ref[2 * hh + 1] / l_ref[2 * hh + 1])
        ms = jnp.mean(o * o, axis=0, keepdims=True)
        outs.append(o * lax.rsqrt(ms + EPS))
    y = jnp.concatenate(outs, axis=0) * g_ref[...] * (1.0 - lam_init)
    o_ref[...] = y.T.astype(BF16)


def _diff_attn(qk, ksa, vt, lq1, lk1, lq2, lk2, g_t, batch, seq, lam_init):
    t = ATT_BLOCK
    nq = seq // t
    ns = 2 * WIDTH_A // HEAD_DIM
    vec = pl.BlockSpec((1, DIFF_DIM), lambda b, qi: (0, 0))
    return pl.pallas_call(
        functools.partial(_diff_attn_kernel, lam_init=lam_init),
        grid=(batch, nq),
        in_specs=_group_specs(_QA_BLK * LANES, WIDTH_A, ns, _VA_BLK * LANES, seq) + [
            vec, vec, vec, vec,
            pl.BlockSpec((WIDTH_A, t), lambda b, qi: (0, 0)),
        ],
        out_specs=pl.BlockSpec((t, WIDTH_A), lambda b, qi: (b * nq + qi, 0)),
        out_shape=jax.ShapeDtypeStruct((batch * seq, WIDTH_A), BF16),
        scratch_shapes=_att_scratch(ns),
        compiler_params=_cparams("parallel", "arbitrary"),
        name="diff_attn",
    )(qk, ksa, vt, lq1, lk1, lq2, lk2, g_t)


def _moba_attn_kernel(q_ref, k_ref, vt_ref, o_ref, kmean_ref, bias_ref, m_ref, l_ref, acc_ref, s_a, s_b, *, nb):
    qi = pl.program_id(1)
    t = ATT_BLOCK
    nh = WIDTH_B // HEAD_DIM

    @pl.when(qi == 0)
    def _():
        kmean_ref[...] = jnp.zeros_like(kmean_ref)
        for j in range(nb):
            kb = k_ref[j * t:(j + 1) * t, :].astype(F32)
            km = jnp.mean(kb, axis=0, keepdims=True)
            for i in range(nh):
                kmean_ref[i * LANES + j:i * LANES + j + 1, :] = km[:, i * LANES:(i + 1) * LANES]

    q = q_ref[...]
    qs = [q[:, hp * LANES:(hp + 1) * LANES] for hp in range(WIDTH_B // LANES)]
    gate_all = _pair_scores(kmean_ref[...].astype(BF16), qs)
    blk = lax.broadcasted_iota(jnp.int32, (nb, t), 0)
    for i in range(nh):
        gate = gate_all[i * LANES:i * LANES + nb]
        cnt = jnp.zeros((nb, t), jnp.int32)
        for r in range(nb):
            gr = gate[r:r + 1, :]
            beats = (gr > gate) | ((gr == gate) & (blk > r))
            live = jnp.where(r < qi, 1, 0)
            cnt = cnt + jnp.where(beats, live, 0)
        sel = (cnt < MOBA_TOPK) & (blk < qi)
        bias_ref[i] = jnp.where(sel, 0.0, NEG_INF)

    def scores(j):
        return _pair_scores(_ks_block(k_ref, j, nh), qs)

    def values(j):
        vt = vt_ref[j]
        return [vt[i * HEAD_DIM:(i + 1) * HEAD_DIM] for i in range(nh)]

    s_all = scores(qi)
    s_a[...] = scores(0)
    mask = _causal_mask_t(t)
    _softmax_first([jnp.where(mask, s_all[i * t:(i + 1) * t], NEG_INF) for i in range(nh)], values(qi),
                   m_ref, l_ref, acc_ref)

    def step(j, s_slot):
        _softmax_next([s_slot[i * t:(i + 1) * t, :] for i in range(nh)], values(j), m_ref, l_ref, acc_ref,
                      col_biases=[bias_ref[i, pl.ds(j, 1), :] for i in range(nh)])

    _pipelined_kv_loop(0, qi, scores, s_a, s_b, step)

    o = jnp.concatenate([acc_ref[i] / l_ref[i] for i in range(nh)], axis=0)
    o_ref[...] = o.T.astype(BF16)


def _moba_attn(qk, ksb, vt, batch, seq):
    t = ATT_BLOCK
    nq = seq // t
    nb = seq // MOBA_BLOCK
    nh = WIDTH_B // HEAD_DIM
    return pl.pallas_call(
        functools.partial(_moba_attn_kernel, nb=nb),
        grid=(batch, nq),
        in_specs=_group_specs(_QB_BLK * LANES, WIDTH_B, nh, _VB_BLK * LANES, seq),
        out_specs=pl.BlockSpec((t, WIDTH_B), lambda b, qi: (b * nq + qi, 0)),
        out_shape=jax.ShapeDtypeStruct((batch * seq, WIDTH_B), BF16),
        scratch_shapes=[pltpu.VMEM((nh * LANES, LANES), F32), pltpu.VMEM((nh, nb, t), F32)] + _att_scratch(nh),
        compiler_params=_cparams("parallel", "arbitrary"),
        name="moba_attn",
    )(qk, ksb, vt)


def _dilated_band_bias():
    c = LANES
    key = np.arange(2 * c)[:, None]
    qry = np.arange(c)[None, :]
    valid = np.where(key < c, qry <= key, key - c <= qry)
    no_prev = valid & (key >= c)
    return np.where(np.stack([valid, no_prev]), 0.0, NEG_INF).astype(np.float32)


def _dilated_units(units, lane):
    n = 2 * LANES
    s_alls = [_nt_dot(_masked_stack(jnp.concatenate([kp, ko], axis=0), lane, HEAD_DIM, 2), q)
              for q, kp, ko, _, _, _ in units]
    stats = []
    for s_all, (_, _, _, _, _, bias) in zip(s_alls, units):
        for hh in range(2):
            s = s_all[hh * n:(hh + 1) * n] + bias
            m = jnp.max(s, axis=0, keepdims=True)
            stats.append((m, jnp.exp2(s - m).astype(BF16)))
    results = []
    for i, (_, _, _, vtp, vto, _) in enumerate(units):
        vt_u = jnp.concatenate([vtp, vto], axis=1)
        outs, lses = [], []
        for hh in range(2):
            m, p = stats[2 * i + hh]
            r = jnp.dot(_with_ones_rows(vt_u[hh * HEAD_DIM:(hh + 1) * HEAD_DIM]), p, preferred_element_type=F32)
            acc, l = r[:HEAD_DIM], r[HEAD_DIM:HEAD_DIM + 1]
            outs.append(acc / l)
            lses.append(jnp.broadcast_to(m + jnp.log2(l), acc.shape))
        results.append((jnp.concatenate(outs, axis=0).T, jnp.concatenate(lses, axis=0).T))
    return results


def _dilated_attn_kernel(q1_ref, k1_ref, k1p_ref, vt1_ref, vt1p_ref,
                         q4_ref, k4_ref, k4p_ref, vt4_ref, vt4p_ref,
                         q16_ref, k16_ref, k16p_ref, v16_ref, v16p_ref, bias_ref, o_ref,
                         o1_ref, l1_ref, o4_ref, l4_ref, o16_ref, l16_ref):
    c = LANES
    lane = lax.broadcasted_iota(jnp.int32, (1, LANES), 1)
    first_tile = jnp.where(pl.program_id(2) == 0, 1, 0)
    bias = bias_ref[0]
    bias_prev_tile = bias_ref[first_tile]

    def rows(ref, chunk):
        return ref[pl.ds(pl.multiple_of(chunk * c, c), c), :]

    def vt_chunk(ref, blk, half):
        return ref[blk, :, half * c:(half + 1) * c]

    def unit1(g, u):
        ch = 4 * g + u
        vt_prev = vt_chunk(vt1_ref, 2 * g - 1, 1) if u == 0 else vt_chunk(vt1_ref, 2 * g + (u - 1) // 2, (u - 1) % 2)
        return (rows(q1_ref, ch), rows(k1_ref, ch - 1), rows(k1_ref, ch), vt_prev,
                vt_chunk(vt1_ref, 2 * g + u // 2, u % 2), bias)

    def store1(g, u, o, lse):
        sl = pl.ds(pl.multiple_of((4 * g + u) * c, c), c)
        o1_ref[sl, :] = o
        l1_ref[sl, :] = lse

    def unit4(g, u):
        ch = 4 * g + u
        return (rows(q4_ref, ch), rows(k4_ref, ch - 4), rows(k4_ref, ch),
                vt_chunk(vt4_ref, 2 * (g - 1) + u // 2, u % 2), vt_chunk(vt4_ref, 2 * g + u // 2, u % 2), bias)

    def store4(g, u, o, lse):
        o4_ref[pl.ds(g * 4 * c + u, c, stride=4), :] = o
        l4_ref[pl.ds(g * 4 * c + u, c, stride=4), :] = lse

    def unit16(g, u):
        ch = 4 * g + u
        return (rows(q16_ref, ch), rows(k16p_ref, ch), rows(k16_ref, ch),
                rows(v16p_ref, ch).astype(F32).T.astype(BF16), rows(v16_ref, ch).astype(F32).T.astype(BF16),
                bias_prev_tile)

    def store16(g, u, o, lse):
        o16_ref[pl.ds(4 * g + u, c, stride=16), :] = o
        l16_ref[pl.ds(4 * g + u, c, stride=16), :] = lse

    first = [(q1_ref[:c, :], k1p_ref[...], k1_ref[:c, :], vt1p_ref[0, :, c:], vt1_ref[0, :, :c], bias_prev_tile)]
    first += [unit1(0, u) for u in range(1, 4)]
    first += [(q4_ref[u * c:(u + 1) * c, :], k4p_ref[u * c:(u + 1) * c, :], k4_ref[u * c:(u + 1) * c, :],
               vt4p_ref[u // 2, :, (u % 2) * c:(u % 2 + 1) * c], vt4_ref[u // 2, :, (u % 2) * c:(u % 2 + 1) * c],
               bias_prev_tile) for u in range(4)]
    res = _dilated_units(first, lane)
    for u in range(4):
        store1(0, u, *res[u])
        store4(0, u, *res[4 + u])

    def group(g, carry):
        res = _dilated_units([unit1(g, u) for u in range(4)] + [unit4(g, u) for u in range(4)], lane)
        for u in range(4):
            store1(g, u, *res[u])
            store4(g, u, *res[4 + u])
        return carry

    lax.fori_loop(1, 4, group, 0)

    def group16(g, carry):
        res = _dilated_units([unit16(2 * g + h, u) for h in range(2) for u in range(4)], lane)
        for h in range(2):
            for u in range(4):
                store16(2 * g + h, u, *res[4 * h + u])
        return carry

    lax.fori_loop(0, 2, group16, 0)

    def merge(i, carry):
        sl = pl.ds(pl.multiple_of(i * c, c), c)
        la, lb, lc = l1_ref[sl, :], l4_ref[sl, :], l16_ref[sl, :]
        top = jnp.maximum(jnp.maximum(la, lb), lc)
        wa, wb, wc = jnp.exp2(la - top), jnp.exp2(lb - top), jnp.exp2(lc - top)
        num = wa * o1_ref[sl, :] + wb * o4_ref[sl, :] + wc * o16_ref[sl, :]
        o_ref[sl, :] = (num / (wa + wb + wc)).astype(BF16)
        return carry

    lax.fori_loop(0, DIL_TILE // c, merge, 0)


def _dilated_attn(qk, vt, qk4, vt4, qkv16, bias, batch, seq):
    tile = DIL_TILE
    nt = seq // tile
    c = LANES
    t = ATT_BLOCK
    nhp = WIDTH_C // LANES
    row = lambda b, hp, ti: b * nt + ti

    def prev(blocks_per_tile):
        return lambda b, hp, ti: jnp.maximum((b * nt + ti) * blocks_per_tile - 1, 0)

    p16, p8, p4, p1 = prev(tile // c), prev(tile // t), prev(4), prev(1)
    full = (tile, LANES)
    vfull = (tile // t, LANES, t)
    in_specs = [
        pl.BlockSpec(full, lambda b, hp, ti: (row(b, hp, ti), _QC_BLK + hp)),
        pl.BlockSpec(full, lambda b, hp, ti: (row(b, hp, ti), _KC_BLK + hp)),
        pl.BlockSpec((c, LANES), lambda b, hp, ti: (p16(b, hp, ti), _KC_BLK + hp)),
        pl.BlockSpec(vfull, lambda b, hp, ti: (row(b, hp, ti), _VC_BLK + hp, 0)),
        pl.BlockSpec((1, LANES, t), lambda b, hp, ti: (p8(b, hp, ti), _VC_BLK + hp, 0)),
        pl.BlockSpec(full, lambda b, hp, ti: (row(b, hp, ti), hp)),
        pl.BlockSpec(full, lambda b, hp, ti: (row(b, hp, ti), nhp + hp)),
        pl.BlockSpec((4 * c, LANES), lambda b, hp, ti: (p4(b, hp, ti), nhp + hp)),
        pl.BlockSpec(vfull, lambda b, hp, ti: (row(b, hp, ti), hp, 0)),
        pl.BlockSpec((2, LANES, t), lambda b, hp, ti: (p4(b, hp, ti), hp, 0)),
        pl.BlockSpec(full, lambda b, hp, ti: (row(b, hp, ti), hp)),
        pl.BlockSpec(full, lambda b, hp, ti: (row(b, hp, ti), nhp + hp)),
        pl.BlockSpec(full, lambda b, hp, ti: (p1(b, hp, ti), nhp + hp)),
        pl.BlockSpec(full, lambda b, hp, ti: (row(b, hp, ti), 2 * nhp + hp)),
        pl.BlockSpec(full, lambda b, hp, ti: (p1(b, hp, ti), 2 * nhp + hp)),
        pl.BlockSpec((2, 2 * c, c), lambda b, hp, ti: (0, 0, 0)),
    ]
    return pl.pallas_call(
        _dilated_attn_kernel,
        grid=(batch, nhp, nt),
        in_specs=in_specs,
        out_specs=pl.BlockSpec(full, lambda b, hp, ti: (row(b, hp, ti), hp)),
        out_shape=jax.ShapeDtypeStruct((batch * seq, WIDTH_C), BF16),
        scratch_shapes=[pltpu.VMEM(full, F32)] * 6,
        compiler_params=_cparams("parallel", "parallel", "arbitrary"),
        name="dilated_attn",
    )(qk, qk, qk, vt, vt, qk4, qk4, qk4, vt4, vt4, qkv16, qkv16, qkv16, qkv16, qkv16, bias)


def _out_mlp_kernel(oa_ref, ob_ref, oc_ref, x_ref, oah_ref, obh_ref, och_ref, xh_ref, wo_ref, gn_ref,
                    wup_ref, cw_ref, cb_ref, wd_ref, gf_ref, o_ref, hcat_ref, u_ref, *, tiles_per_seq, final_norm):
    i = pl.program_id(0)
    tm = x_ref.shape[0]
    tf = FF_TILE
    nf = D_FF // tf
    seq_start = (i % tiles_per_seq) == 0

    def mix_norm(oa, ob, oc, x):
        acc = jnp.dot(oa, wo_ref[:WIDTH_A, :], preferred_element_type=F32)
        acc = acc + jnp.dot(ob, wo_ref[WIDTH_A:WIDTH_A + WIDTH_B, :], preferred_element_type=F32)
        acc = acc + jnp.dot(oc, wo_ref[WIDTH_A + WIDTH_B:, :], preferred_element_type=F32)
        xn = x + acc
        ms = jnp.mean(xn * xn, axis=-1, keepdims=True)
        return xn, (xn * lax.rsqrt(ms + EPS) * gn_ref[...]).astype(BF16)

    _, h_halo = mix_norm(oah_ref[...], obh_ref[...], och_ref[...], xh_ref[...])
    hcat_ref[:HALO, :] = jnp.where(seq_start, jnp.zeros_like(h_halo), h_halo)
    xn, h = mix_norm(oa_ref[...], ob_ref[...], oc_ref[...], x_ref[...])
    hcat_ref[HALO:, :] = h
    o_ref[...] = xn

    def cols(b, c):
        return slice(b * D_FF + c * tf, b * D_FF + (c + 1) * tf)

    def up(c):
        for b in range(2):
            u_ref[c % 2, b] = jnp.dot(hcat_ref[...], wup_ref[:, cols(b, c)], preferred_element_type=F32)

    def down(c):
        def conv(b):
            cw = cw_ref[:, cols(b, c)]
            return (cw[2:3] * u_ref[c % 2, b, HALO:HALO + tm, :]
                    + cw[1:2] * u_ref[c % 2, b, HALO - 1:HALO - 1 + tm, :]
                    + cw[0:1] * u_ref[c % 2, b, HALO - 2:HALO - 2 + tm, :]
                    + cb_ref[:, cols(b, c)])

        gate = conv(0)
        val = conv(1)
        act = (gate * jax.nn.sigmoid(gate) * val).astype(BF16)
        o_ref[...] += jnp.dot(act, wd_ref[c * tf:(c + 1) * tf, :], preferred_element_type=F32)

    up(0)
    for c in range(nf):
        if c + 1 < nf:
            up(c + 1)
        down(c)

    if final_norm:
        y = o_ref[...]
        ms = jnp.mean(y * y, axis=-1, keepdims=True)
        o_ref[...] = y * lax.rsqrt(ms + EPS) * gf_ref[...]


def _out_mlp(oa, ob, oc, x2d, w_out, g_ffn, w_up, conv_w, conv_b, w_down, g_final, seq, final_norm):
    m = x2d.shape[0]
    tm, tf = ROW_TILE, FF_TILE
    hb = tm // HALO
    row = lambda i: (i, 0)
    halo = lambda i: (jnp.maximum(i * hb - 1, 0), 0)
    whole = lambda i: (0, 0)
    once = pl.Buffered(1)
    return pl.pallas_call(
        functools.partial(_out_mlp_kernel, tiles_per_seq=seq // tm, final_norm=final_norm),
        grid=(m // tm,),
        in_specs=[
            pl.BlockSpec((tm, WIDTH_A), row),
            pl.BlockSpec((tm, WIDTH_B), row),
            pl.BlockSpec((tm, WIDTH_C), row),
            pl.BlockSpec((tm, D_MODEL), row),
            pl.BlockSpec((HALO, WIDTH_A), halo),
            pl.BlockSpec((HALO, WIDTH_B), halo),
            pl.BlockSpec((HALO, WIDTH_C), halo),
            pl.BlockSpec((HALO, D_MODEL), halo),
            pl.BlockSpec((D_MODEL, D_MODEL), whole, pipeline_mode=once),
            pl.BlockSpec((1, D_MODEL), whole),
            pl.BlockSpec((D_MODEL, 2 * D_FF), whole, pipeline_mode=once),
            pl.BlockSpec((3, 2 * D_FF), whole, pipeline_mode=once),
            pl.BlockSpec((1, 2 * D_FF), whole, pipeline_mode=once),
            pl.BlockSpec((D_FF, D_MODEL), whole, pipeline_mode=once),
            pl.BlockSpec((1, D_MODEL), whole),
        ],
        out_specs=pl.BlockSpec((tm, D_MODEL), row),
        out_shape=jax.ShapeDtypeStruct((m, D_MODEL), F32),
        scratch_shapes=[
            pltpu.VMEM((HALO + tm, D_MODEL), BF16),
            pltpu.VMEM((2, 2, HALO + tm, tf), F32),
        ],
        compiler_params=_cparams("parallel"),
        name="out_mlp",
    )(oa, ob, oc, x2d, oa, ob, oc, x2d, w_out, g_ffn, w_up, conv_w, conv_b, w_down, g_final)


def _split_w_in(w):
    a, b = WIDTH_A, WIDTH_B
    qk_cols = [w[:, 0:2 * a], w[:, 3 * a:3 * a + 2 * b], w[:, 3 * a + 3 * b:3 * a + 3 * b + 2 * WIDTH_C]]
    v_cols = [w[:, 2 * a:3 * a], w[:, 3 * a + 2 * b:3 * a + 3 * b], w[:, 3 * a + 3 * b + 2 * WIDTH_C:]]
    return (jnp.concatenate(qk_cols, axis=1).astype(BF16),
            jnp.concatenate(v_cols, axis=1).T.astype(BF16))


def kernel(x, positions, g_mix, w_in, w_out, lambda_q1, lambda_k1, lambda_q2, lambda_k2, g_diff, g_ffn,
           w_up, conv_w, conv_b, w_down, g_final):
    batch, seq, d_model = x.shape
    depth = g_mix.shape[0]
    assert d_model == D_MODEL and seq % DIL_TILE == 0 and seq % MOBA_BLOCK == 0
    m = batch * seq
    tab = _rope_tables(positions)
    bias = jnp.asarray(_dilated_band_bias())
    xf = x.reshape(m, d_model)
    for layer in range(depth):
        lam_init = 0.8 - 0.6 * math.exp(-0.3 * layer)
        wqk, wvt = _split_w_in(w_in[layer])
        qk, ksa, ksb, vt, qk4, vt4, qkv16 = _proj_in(xf, g_mix[layer][None, :], wqk, wvt, tab)
        g_t = jnp.broadcast_to(jnp.tile(g_diff[layer], WIDTH_A // HEAD_DIM)[:, None], (WIDTH_A, ATT_BLOCK))
        o_a = _diff_attn(qk, ksa, vt, lambda_q1[layer][None, :], lambda_k1[layer][None, :],
                         lambda_q2[layer][None, :], lambda_k2[layer][None, :], g_t, batch, seq, lam_init)
        o_b = _moba_attn(qk, ksb, vt, batch, seq)
        o_c = _dilated_attn(qk, vt, qk4, vt4, qkv16, bias, batch, seq)
        xf = _out_mlp(o_a, o_b, o_c, xf, w_out[layer].astype(BF16), g_ffn[layer][None, :],
                      w_up[layer].astype(BF16), conv_w[layer], conv_b[layer][None, :],
                      w_down[layer].astype(BF16), g_final[None, :], seq, layer == depth - 1)
    return xf.reshape(batch, seq, d_model)
```

```python
import functools
import math

import numpy as np
import jax
import jax.numpy as jnp
from jax import lax
from jax.experimental import pallas as pl
from jax.experimental.pallas import tpu as pltpu

F32 = jnp.float32
BF16 = jnp.bfloat16

D_MODEL = 1024
HEAD_DIM = 64
DIFF_DIM = 32
WIDTH_A = 256
WIDTH_B = 256
WIDTH_C = 512
ROPE_THETA = 500000.0
MOBA_BLOCK = 256
MOBA_TOPK = 3
DILATED_PATTERNS = ((128, 1), (512, 4), (2048, 16))
D_FF = 2816
EPS = 1e-6
NEG_INF = -1e30
LOG2E = math.log2(math.e)

LANES = 128
ATT_BLOCK = 256
DIL_TILE = 2048
ROW_TILE = 512
FF_TILE = 256
HALO = 16
ONES_ROWS = 16
VMEM_LIMIT = 48 * 1024 * 1024

_QK_SEGMENTS = (
    (0, 256, "qa", ("qk", 0)), (256, 256, "ka", ("ks", 0)),
    (512, 256, "q", ("qk", 256)), (768, 256, "k", ("ks", 1)),
    (1024, 512, "q", ("qk", 512)), (1536, 512, "k", ("qk", 1024)),
)
_W_QK_WIDTH = 2048
_QK_WIDTH = 1536
_V_WIDTH = 1024
_KS_STREAMS = (4, 2)
_QA_BLK, _QB_BLK, _QC_BLK, _KC_BLK = 0, 2, 4, 8
_VA_BLK, _VB_BLK, _VC_BLK = 0, 2, 4
_W_QC_START = 1024


def _cparams(*sem):
    return pltpu.CompilerParams(dimension_semantics=sem, vmem_limit_bytes=VMEM_LIMIT)


def _rope_table_kernel(pos_ref, tab_ref):
    pos = pos_ref[...].astype(F32)
    tr = pos.shape[0]
    lane = lax.broadcasted_iota(jnp.int32, (1, LANES), 1)
    half, rot = HEAD_DIM // 8, HEAD_DIM // 4
    d = lane & (HEAD_DIM - 1)
    j = jnp.where(d < half, d, d - half)
    expo = -(j.astype(F32)) * 2.0 / rot
    inv = jnp.where(d < rot, jnp.power(jnp.float32(ROPE_THETA), expo), 0.0)
    ang = pos * inv
    cos_c, sin_c = jnp.cos(ang), jnp.sin(ang)
    da = lane & (DIFF_DIM - 1)
    half_a = DIFF_DIM // 8
    ja = jnp.where(da < half_a, da, da - half_a)
    src = jnp.where(da < 2 * half_a, half + 2 * ja, rot)
    idx = jnp.broadcast_to(src, (tr, LANES))
    cos_a = jnp.take_along_axis(cos_c, idx, axis=1)
    sin_a = jnp.take_along_axis(sin_c, idx, axis=1)
    tab_ref[:, :LANES] = cos_a
    tab_ref[:, LANES:2 * LANES] = cos_c
    tab_ref[:, 2 * LANES:3 * LANES] = sin_a * jnp.where(da < half_a, -1.0, 1.0).astype(F32)
    tab_ref[:, 3 * LANES:] = sin_c * jnp.where(d < half, -1.0, 1.0).astype(F32)


def _rope_tables(positions):
    m = positions.size
    tr = ROW_TILE
    return pl.pallas_call(
        _rope_table_kernel,
        grid=(m // tr,),
        in_specs=[pl.BlockSpec((tr, 1), lambda i: (i, 0))],
        out_specs=pl.BlockSpec((tr, 4 * LANES), lambda i: (i, 0)),
        out_shape=jax.ShapeDtypeStruct((m, 4 * LANES), F32),
        compiler_params=_cparams("parallel"),
        name="rope_tables",
    )(positions.reshape(m, 1))


def _nt_dot(a, b):
    return lax.dot_general(a, b, (((1,), (1,)), ((), ())), preferred_element_type=F32)


def _proj_in_kernel(x_ref, g_ref, wqk_ref, wvt_ref, tab_ref, qk_ref, ksa_ref, ksb_ref, vt_ref, qk4_ref, vt4_ref,
                    qkv16_ref, cs_ref, c4_ref):
    tm = x_ref.shape[0]
    x = x_ref[...]
    ms = jnp.mean(x * x, axis=-1, keepdims=True)
    h = (x * lax.rsqrt(ms + EPS) * g_ref[...]).astype(BF16)
    lane = lax.broadcasted_iota(jnp.int32, (1, LANES), 1)
    ks_refs = (ksa_ref, ksb_ref)
    slab = 0
    for start, width, kind, dest in _QK_SEGMENTS:
        acc = jnp.dot(h, wqk_ref[:, start:start + width], preferred_element_type=F32)
        narrow = kind.endswith("a")
        dim = DIFF_DIM if narrow else HEAD_DIM
        half = dim // 8
        off = 0 if narrow else LANES
        cos_t = tab_ref[:, off:off + LANES]
        sin_t = tab_ref[:, 2 * LANES + off:3 * LANES + off]
        upper = (lane & (dim - 1)) >= half
        scale = dim ** -0.5 * LOG2E if kind.startswith("q") else None
        for c in range(width // LANES):
            xs = acc[:, c * LANES:(c + 1) * LANES]
            partner = jnp.where(upper, pltpu.roll(xs, half, 1), pltpu.roll(xs, LANES - half, 1))
            y = xs * cos_t + partner * sin_t
            if scale is not None:
                y = y * scale
            if dest[0] == "qk":
                col = dest[1] + c * LANES
                qk_ref[:, col:col + LANES] = y.astype(BF16)
            else:
                n = _KS_STREAMS[dest[1]]
                yb = y.astype(BF16)
                for i in range(n):
                    keep = (lane >= i * dim) & (lane < (i + 1) * dim)
                    col = (c * n + i) * LANES
                    ks_refs[dest[1]][:, col:col + LANES] = jnp.where(keep, yb, jnp.zeros_like(yb))
            if start >= _W_QC_START:
                cs_ref[slab] = y
                slab += 1
    t = ATT_BLOCK
    for r in range(_V_WIDTH // t):
        vt = _nt_dot(wvt_ref[r * t:(r + 1) * t, :], h)
        vtb = vt.astype(BF16)
        for c in range(vt_ref.shape[0]):
            vt_ref[c, r * t:(r + 1) * t, :] = vtb[:, c * t:(c + 1) * t]
        if r * t >= _VC_BLK * LANES:
            v_nat = vt.T
            for c in range(t // LANES):
                cs_ref[slab] = v_nat[:, c * LANES:(c + 1) * LANES]
                slab += 1
    n_qk = 2 * WIDTH_C // LANES
    n_all = 3 * WIDTH_C // LANES
    assert slab == n_all
    quarter = tm // 4
    for s in range(n_all):
        rows = [cs_ref[s, pl.ds(r, quarter, stride=4), :] for r in range(4)]
        for r in range(4):
            c4_ref[s, r * quarter:(r + 1) * quarter, :] = rows[r]
        if s < n_qk:
            for r in range(4):
                qk4_ref[r * quarter:(r + 1) * quarter, s * LANES:(s + 1) * LANES] = rows[r].astype(BF16)
        else:
            v4t = jnp.concatenate(rows, axis=0).T.astype(BF16)
            vs = s - n_qk
            for c in range(vt4_ref.shape[0]):
                vt4_ref[c, vs * LANES:(vs + 1) * LANES, :] = v4t[:, c * t:(c + 1) * t]
    per = tm // 16
    for s in range(n_all):
        for r1 in range(4):
            for r2 in range(4):
                qkv16_ref[0, 4 * r2 + r1, 0, :, s * LANES:(s + 1) * LANES] = (
                    c4_ref[s, pl.ds(r1 * quarter + r2, per, stride=4), :].astype(BF16))


def _proj_in(x2d, g, wqk, wvt, tab):
    m = x2d.shape[0]
    tm, t = ROW_TILE, ATT_BLOCK
    assert tm == DIL_TILE // 4 and tm == 4 * LANES
    per = tm // 16
    n_all = 3 * WIDTH_C // LANES
    outs = pl.pallas_call(
        _proj_in_kernel,
        grid=(m // tm,),
        in_specs=[
            pl.BlockSpec((tm, D_MODEL), lambda i: (i, 0)),
            pl.BlockSpec((1, D_MODEL), lambda i: (0, 0)),
            pl.BlockSpec((D_MODEL, _W_QK_WIDTH), lambda i: (0, 0)),
            pl.BlockSpec((_V_WIDTH, D_MODEL), lambda i: (0, 0)),
            pl.BlockSpec((tm, 4 * LANES), lambda i: (i, 0)),
        ],
        out_specs=[
            pl.BlockSpec((tm, _QK_WIDTH), lambda i: (i, 0)),
            pl.BlockSpec((tm, WIDTH_A * _KS_STREAMS[0]), lambda i: (i, 0)),
            pl.BlockSpec((tm, WIDTH_B * _KS_STREAMS[1]), lambda i: (i, 0)),
            pl.BlockSpec((tm // t, _V_WIDTH, t), lambda i: (i, 0, 0)),
            pl.BlockSpec((tm, 2 * WIDTH_C), lambda i: (i, 0)),
            pl.BlockSpec((tm // t, WIDTH_C, t), lambda i: (i, 0, 0)),
            pl.BlockSpec((1, 16, 1, per, 3 * WIDTH_C), lambda i: (i // 4, 0, i % 4, 0, 0)),
        ],
        out_shape=[
            jax.ShapeDtypeStruct((m, _QK_WIDTH), BF16),
            jax.ShapeDtypeStruct((m, WIDTH_A * _KS_STREAMS[0]), BF16),
            jax.ShapeDtypeStruct((m, WIDTH_B * _KS_STREAMS[1]), BF16),
            jax.ShapeDtypeStruct((m // t, _V_WIDTH, t), BF16),
            jax.ShapeDtypeStruct((m, 2 * WIDTH_C), BF16),
            jax.ShapeDtypeStruct((m // t, WIDTH_C, t), BF16),
            jax.ShapeDtypeStruct((m // DIL_TILE, 16, 4, per, 3 * WIDTH_C), BF16),
        ],
        scratch_shapes=[pltpu.VMEM((n_all, tm, LANES), F32)] * 2,
        compiler_params=_cparams("parallel"),
        name="proj_in",
    )(x2d, g, wqk, wvt, tab)
    qk, ksa, ksb, vt, qk4, vt4, qkv16 = outs
    return qk, ksa, ksb, vt, qk4, vt4, qkv16.reshape(m, 3 * WIDTH_C)


def _with_ones_rows(vt):
    return jnp.concatenate([vt, jnp.ones((ONES_ROWS, vt.shape[1]), vt.dtype)], axis=0)


def _softmax_first(ss, vts, m_ref, l_ref, acc_ref):
    ms = [jnp.max(s, axis=0, keepdims=True) for s in ss]
    ps = [jnp.exp2(s - m).astype(BF16) for s, m in zip(ss, ms)]
    for i, (p, m, vt) in enumerate(zip(ps, ms, vts)):
        r = jnp.dot(_with_ones_rows(vt), p, preferred_element_type=F32)
        m_ref[i] = m
        l_ref[i] = r[HEAD_DIM:HEAD_DIM + 1]
        acc_ref[i] = r[:HEAD_DIM]


def _softmax_next(ss, vts, m_ref, l_ref, acc_ref, col_biases=None):
    n = len(ss)
    m_prevs = [m_ref[i] for i in range(n)]
    m_blks = [jnp.max(s, axis=0, keepdims=True) for s in ss]
    if col_biases is not None:
        m_blks = [m + b for m, b in zip(m_blks, col_biases)]
    m_news = [jnp.maximum(a, b) for a, b in zip(m_prevs, m_blks)]
    shifts = m_news if col_biases is None else [m - b for m, b in zip(m_news, col_biases)]
    ps = [jnp.exp2(s - sh).astype(BF16) for s, sh in zip(ss, shifts)]
    for i in range(n):
        alpha = jnp.exp2(m_prevs[i] - m_news[i])
        r = jnp.dot(_with_ones_rows(vts[i]), ps[i], preferred_element_type=F32)
        m_ref[i] = m_news[i]
        l_ref[i] = alpha * l_ref[i] + r[HEAD_DIM:HEAD_DIM + 1]
        acc_ref[i] = alpha * acc_ref[i] + r[:HEAD_DIM]


def _causal_mask_t(t):
    key = lax.broadcasted_iota(jnp.int32, (t, t), 0)
    qry = lax.broadcasted_iota(jnp.int32, (t, t), 1)
    return key <= qry


def _masked_stack(k_blk, lane, width, n):
    zero = jnp.zeros_like(k_blk)
    return jnp.concatenate(
        [jnp.where((lane >= i * width) & (lane < (i + 1) * width), k_blk, zero) for i in range(n)], axis=0)


def _ks_block(ks_ref, j, n):
    blk = ks_ref[pl.ds(pl.multiple_of(j * ATT_BLOCK, ATT_BLOCK), ATT_BLOCK), :]
    return jnp.concatenate([blk[:, i * LANES:(i + 1) * LANES] for i in range(n)], axis=0)


def _att_scratch(n_streams):
    t = ATT_BLOCK
    return [
        pltpu.VMEM((n_streams, 1, t), F32),
        pltpu.VMEM((n_streams, 1, t), F32),
        pltpu.VMEM((n_streams, HEAD_DIM, t), F32),
        pltpu.VMEM((n_streams * t, t), F32),
        pltpu.VMEM((n_streams * t, t), F32),
    ]


def _pipelined_kv_loop(lo, hi, scores, s_a, s_b, step):
    n = hi - lo

    def body(i, carry):
        j = lo + 2 * i
        s_b[...] = scores(j + 1)
        step(j, s_a)
        s_a[...] = scores(j + 2)
        step(j + 1, s_b)
        return carry

    lax.fori_loop(0, n // 2, body, 0)

    @pl.when(n % 2 == 1)
    def _():
        step(hi - 1, s_a)


def _group_specs(q_col, width, n_copies, vt_row, seq):
    t = ATT_BLOCK
    nq = seq // t
    assert q_col % width == 0 and vt_row % width == 0
    return [
        pl.BlockSpec((t, width), lambda b, qi: (b * nq + qi, q_col // width)),
        pl.BlockSpec((seq, n_copies * LANES), lambda b, qi: (b, 0)),
        pl.BlockSpec((nq, width, t), lambda b, qi: (b, vt_row // width, 0)),
    ]


def _pair_scores(stacked, qs):
    rows = stacked.shape[0] // len(qs)
    return jnp.concatenate([_nt_dot(stacked[hp * rows:(hp + 1) * rows], qs[hp]) for hp in range(len(qs))], axis=0)


def _diff_attn_kernel(q_ref, k_ref, vt_ref, lq1_ref, lk1_ref, lq2_ref, lk2_ref, g_ref, o_ref,
                      m_ref, l_ref, acc_ref, s_a, s_b, *, lam_init):
    qi = pl.program_id(1)
    t = ATT_BLOCK
    nh = WIDTH_A // HEAD_DIM
    ns = 2 * nh
    q = q_ref[...]
    qs = [q[:, hp * LANES:(hp + 1) * LANES] for hp in range(WIDTH_A // LANES)]

    def scores(j):
        return _pair_scores(_ks_block(k_ref, j, ns), qs)

    def values(j):
        vt = vt_ref[j]
        return [vt[(i // 2) * HEAD_DIM:(i // 2 + 1) * HEAD_DIM] for i in range(ns)]

    s_all = scores(qi)
    s_a[...] = scores(0)
    mask = _causal_mask_t(t)
    _softmax_first([jnp.where(mask, s_all[i * t:(i + 1) * t], NEG_INF) for i in range(ns)], values(qi),
                   m_ref, l_ref, acc_ref)

    def step(j, s_slot):
        _softmax_next([s_slot[i * t:(i + 1) * t, :] for i in range(ns)], values(j), m_ref, l_ref, acc_ref)

    _pipelined_kv_loop(0, qi, scores, s_a, s_b, step)

    lam = (jnp.exp(jnp.sum(lq1_ref[...] * lk1_ref[...], axis=1, keepdims=True))
           - jnp.exp(jnp.sum(lq2_ref[...] * lk2_ref[...], axis=1, keepdims=True)) + lam_init)
    outs = []
    for hh in range(nh):
        o = acc_ref[2 * hh] / l_ref[2 * hh] - lam * (acc_ref[2 * hh + 1] / l_ref[2 * hh + 1])
        ms = jnp.mean(o * o, axis=0, keepdims=True)
        outs.append(o * lax.rsqrt(ms + EPS))
    y = jnp.concatenate(outs, axis=0) * g_ref[...] * (1.0 - lam_init)
    o_ref[...] = y.T.astype(BF16)


def _diff_attn(qk, ksa, vt, lq1, lk1, lq2, lk2, g_t, batch, seq, lam_init):
    t = ATT_BLOCK
    nq = seq // t
    ns = 2 * WIDTH_A // HEAD_DIM
    vec = pl.BlockSpec((1, DIFF_DIM), lambda b, qi: (0, 0))
    return pl.pallas_call(
        functools.partial(_diff_attn_kernel, lam_init=lam_init),
        grid=(batch, nq),
        in_specs=_group_specs(_QA_BLK * LANES, WIDTH_A, ns, _VA_BLK * LANES, seq) + [
            vec, vec, vec, vec,
            pl.BlockSpec((WIDTH_A, t), lambda b, qi: (0, 0)),
        ],
        out_specs=pl.BlockSpec((t, WIDTH_A), lambda b, qi: (b * nq + qi, 0)),
        out_shape=jax.ShapeDtypeStruct((batch * seq, WIDTH_A), BF16),
        scratch_shapes=_att_scratch(ns),
        compiler_params=_cparams("parallel", "arbitrary"),
        name="diff_attn",
    )(qk, ksa, vt, lq1, lk1, lq2, lk2, g_t)


def _moba_attn_kernel(q_ref, k_ref, vt_ref, o_ref, kmean_ref, bias_ref, m_ref, l_ref, acc_ref, s_a, s_b, *, nb):
    qi = pl.program_id(1)
    t = ATT_BLOCK
    nh = WIDTH_B // HEAD_DIM

    @pl.when(qi == 0)
    def _():
        kmean_ref[...] = jnp.zeros_like(kmean_ref)
        for j in range(nb):
            kb = k_ref[j * t:(j + 1) * t, :].astype(F32)
            km = jnp.mean(kb, axis=0, keepdims=True)
            for i in range(nh):
                kmean_ref[i * LANES + j:i * LANES + j + 1, :] = km[:, i * LANES:(i + 1) * LANES]

    q = q_ref[...]
    qs = [q[:, hp * LANES:(hp + 1) * LANES] for hp in range(WIDTH_B // LANES)]
    gate_all = _pair_scores(kmean_ref[...].astype(BF16), qs)
    blk = lax.broadcasted_iota(jnp.int32, (nb, t), 0)
    for i in range(nh):
        gate = gate_all[i * LANES:i * LANES + nb]
        cnt = jnp.zeros((nb, t), jnp.int32)
        for r in range(nb):
            gr = gate[r:r + 1, :]
            beats = (gr > gate) | ((gr == gate) & (blk > r))
            live = jnp.where(r < qi, 1, 0)
            cnt = cnt + jnp.where(beats, live, 0)
        sel = (cnt < MOBA_TOPK) & (blk < qi)
        bias_ref[i] = jnp.where(sel, 0.0, NEG_INF)

    def scores(j):
        return _pair_scores(_ks_block(k_ref, j, nh), qs)

    def values(j):
        vt = vt_ref[j]
        return [vt[i * HEAD_DIM:(i + 1) * HEAD_DIM] for i in range(nh)]

    s_all = scores(qi)
    s_a[...] = scores(0)
    mask = _causal_mask_t(t)
    _softmax_first([jnp.where(mask, s_all[i * t:(i + 1) * t], NEG_INF) for i in range(nh)], values(qi),
                   m_ref, l_ref, acc_ref)

    def step(j, s_slot):
        _softmax_next([s_slot[i * t:(i + 1) * t, :] for i in range(nh)], values(j), m_ref, l_ref, acc_ref,
                      col_biases=[bias_ref[i, pl.ds(j, 1), :] for i in range(nh)])

    _pipelined_kv_loop(0, qi, scores, s_a, s_b, step)

    o = jnp.concatenate([acc_ref[i] / l_ref[i] for i in range(nh)], axis=0)
    o_ref[...] = o.T.astype(BF16)


def _moba_attn(qk, ksb, vt, batch, seq):
    t = ATT_BLOCK
    nq = seq // t
    nb = seq // MOBA_BLOCK
    nh = WIDTH_B // HEAD_DIM
    return pl.pallas_call(
        functools.partial(_moba_attn_kernel, nb=nb),
        grid=(batch, nq),
        in_specs=_group_specs(_QB_BLK * LANES, WIDTH_B, nh, _VB_BLK * LANES, seq),
        out_specs=pl.BlockSpec((t, WIDTH_B), lambda b, qi: (b * nq + qi, 0)),
        out_shape=jax.ShapeDtypeStruct((batch * seq, WIDTH_B), BF16),
        scratch_shapes=[pltpu.VMEM((nh * LANES, LANES), F32), pltpu.VMEM((nh, nb, t), F32)] + _att_scratch(nh),
        compiler_params=_cparams("parallel", "arbitrary"),
        name="moba_attn",
    )(qk, ksb, vt)


def _dilated_band_bias():
    c = LANES
    key = np.arange(2 * c)[:, None]
    qry = np.arange(c)[None, :]
    valid = np.where(key < c, qry <= key, key - c <= qry)
    no_prev = valid & (key >= c)
    return np.where(np.stack([valid, no_prev]), 0.0, NEG_INF).astype(np.float32)


def _dilated_units(units, lane):
    n = 2 * LANES
    s_alls = [_nt_dot(_masked_stack(jnp.concatenate([kp, ko], axis=0), lane, HEAD_DIM, 2), q)
              for q, kp, ko, _, _, _ in units]
    stats = []
    for s_all, (_, _, _, _, _, bias) in zip(s_alls, units):
        for hh in range(2):
            s = s_all[hh * n:(hh + 1) * n] + bias
            m = jnp.max(s, axis=0, keepdims=True)
            stats.append((m, jnp.exp2(s - m).astype(BF16)))
    results = []
    for i, (_, _, _, vtp, vto, _) in enumerate(units):
        vt_u = jnp.concatenate([vtp, vto], axis=1)
        outs, lses = [], []
        for hh in range(2):
            m, p = stats[2 * i + hh]
            r = jnp.dot(_with_ones_rows(vt_u[hh * HEAD_DIM:(hh + 1) * HEAD_DIM]), p, preferred_element_type=F32)
            acc, l = r[:HEAD_DIM], r[HEAD_DIM:HEAD_DIM + 1]
            outs.append(acc / l)
            lses.append(jnp.broadcast_to(m + jnp.log2(l), acc.shape))
        results.append((jnp.concatenate(outs, axis=0).T, jnp.concatenate(lses, axis=0).T))
    return results


def _dilated_attn_kernel(q1_ref, k1_ref, k1p_ref, vt1_ref, vt1p_ref,
                         q4_ref, k4_ref, k4p_ref, vt4_ref, vt4p_ref,
                         q16_ref, k16_ref, k16p_ref, v16_ref, v16p_ref, bias_ref, o_ref,
                         o1_ref, l1_ref, o4_ref, l4_ref, o16_ref, l16_ref):
    c = LANES
    lane = lax.broadcasted_iota(jnp.int32, (1, LANES), 1)
    first_tile = jnp.where(pl.program_id(2) == 0, 1, 0)
    bias = bias_ref[0]
    bias_prev_tile = bias_ref[first_tile]

    def rows(ref, chunk):
        return ref[pl.ds(pl.multiple_of(chunk * c, c), c), :]

    def vt_chunk(ref, blk, half):
        return ref[blk, :, half * c:(half + 1) * c]

    def unit1(g, u):
        ch = 4 * g + u
        vt_prev = vt_chunk(vt1_ref, 2 * g - 1, 1) if u == 0 else vt_chunk(vt1_ref, 2 * g + (u - 1) // 2, (u - 1) % 2)
        return (rows(q1_ref, ch), rows(k1_ref, ch - 1), rows(k1_ref, ch), vt_prev,
                vt_chunk(vt1_ref, 2 * g + u // 2, u % 2), bias)

    def store1(g, u, o, lse):
        sl = pl.ds(pl.multiple_of((4 * g + u) * c, c), c)
        o1_ref[sl, :] = o
        l1_ref[sl, :] = lse

    def unit4(g, u):
        ch = 4 * g + u
        return (rows(q4_ref, ch), rows(k4_ref, ch - 4), rows(k4_ref, ch),
                vt_chunk(vt4_ref, 2 * (g - 1) + u // 2, u % 2), vt_chunk(vt4_ref, 2 * g + u // 2, u % 2), bias)

    def store4(g, u, o, lse):
        o4_ref[pl.ds(g * 4 * c + u, c, stride=4), :] = o
        l4_ref[pl.ds(g * 4 * c + u, c, stride=4), :] = lse

    def unit16(g, u):
        ch = 4 * g + u
        return (rows(q16_ref, ch), rows(k16p_ref, ch), rows(k16_ref, ch),
                rows(v16p_ref, ch).astype(F32).T.astype(BF16), rows(v16_ref, ch).astype(F32).T.astype(BF16),
                bias_prev_tile)

    def store16(g, u, o, lse):
        o16_ref[pl.ds(4 * g + u, c, stride=16), :] = o
        l16_ref[pl.ds(4 * g + u, c, stride=16), :] = lse

    first = [(q1_ref[:c, :], k1p_ref[...], k1_ref[:c, :], vt1p_ref[0, :, c:], vt1_ref[0, :, :c], bias_prev_tile)]
    first += [unit1(0, u) for u in range(1, 4)]
    first += [(q4_ref[u * c:(u + 1) * c, :], k4p_ref[u * c:(u + 1) * c, :], k4_ref[u * c:(u + 1) * c, :],
               vt4p_ref[u // 2, :, (u % 2) * c:(u % 2 + 1) * c], vt4_ref[u // 2, :, (u % 2) * c:(u % 2 + 1) * c],
               bias_prev_tile) for u in range(4)]
    res = _dilated_units(first, lane)
    for u in range(4):
        store1(0, u, *res[u])
        store4(0, u, *res[4 + u])

    def group(g, carry):
        res = _dilated_units([unit1(g, u) for u in range(4)] + [unit4(g, u) for u in range(4)], lane)
        for u in range(4):
            store1(g, u, *res[u])
            store4(g, u, *res[4 + u])
        return carry

    lax.fori_loop(1, 4, group, 0)

    def group16(g, carry):
        res = _dilated_units([unit16(2 * g + h, u) for h in range(2) for u in range(4)], lane)
        for h in range(2):
            for u in range(4):
                store16(2 * g + h, u, *res[4 * h + u])
        return carry

    lax.fori_loop(0, 2, group16, 0)

    def merge(i, carry):
        sl = pl.ds(pl.multiple_of(i * c, c), c)
        la, lb, lc = l1_ref[sl, :], l4_ref[sl, :], l16_ref[sl, :]
        top = jnp.maximum(jnp.maximum(la, lb), lc)
        wa, wb, wc = jnp.exp2(la - top), jnp.exp2(lb - top), jnp.exp2(lc - top)
        num = wa * o1_ref[sl, :] + wb * o4_ref[sl, :] + wc * o16_ref[sl, :]
        o_ref[sl, :] = (num / (wa + wb + wc)).astype(BF16)
        return carry

    lax.fori_loop(0, DIL_TILE // c, merge, 0)


def _dilated_attn(qk, vt, qk4, vt4, qkv16, bias, batch, seq):
    tile = DIL_TILE
    nt = seq // tile
    c = LANES
    t = ATT_BLOCK
    nhp = WIDTH_C // LANES
    row = lambda b, hp, ti: b * nt + ti

    def prev(blocks_per_tile):
        return lambda b, hp, ti: jnp.maximum((b * nt + ti) * blocks_per_tile - 1, 0)

    p16, p8, p4, p1 = prev(tile // c), prev(tile // t), prev(4), prev(1)
    full = (tile, LANES)
    vfull = (tile // t, LANES, t)
    in_specs = [
        pl.BlockSpec(full, lambda b, hp, ti: (row(b, hp, ti), _QC_BLK + hp)),
        pl.BlockSpec(full, lambda b, hp, ti: (row(b, hp, ti), _KC_BLK + hp)),
        pl.BlockSpec((c, LANES), lambda b, hp, ti: (p16(b, hp, ti), _KC_BLK + hp)),
        pl.BlockSpec(vfull, lambda b, hp, ti: (row(b, hp, ti), _VC_BLK + hp, 0)),
        pl.BlockSpec((1, LANES, t), lambda b, hp, ti: (p8(b, hp, ti), _VC_BLK + hp, 0)),
        pl.BlockSpec(full, lambda b, hp, ti: (row(b, hp, ti), hp)),
        pl.BlockSpec(full, lambda b, hp, ti: (row(b, hp, ti), nhp + hp)),
        pl.BlockSpec((4 * c, LANES), lambda b, hp, ti: (p4(b, hp, ti), nhp + hp)),
        pl.BlockSpec(vfull, lambda b, hp, ti: (row(b, hp, ti), hp, 0)),
        pl.BlockSpec((2, LANES, t), lambda b, hp, ti: (p4(b, hp, ti), hp, 0)),
        pl.BlockSpec(full, lambda b, hp, ti: (row(b, hp, ti), hp)),
        pl.BlockSpec(full, lambda b, hp, ti: (row(b, hp, ti), nhp + hp)),
        pl.BlockSpec(full, lambda b, hp, ti: (p1(b, hp, ti), nhp + hp)),
        pl.BlockSpec(full, lambda b, hp, ti: (row(b, hp, ti), 2 * nhp + hp)),
        pl.BlockSpec(full, lambda b, hp, ti: (p1(b, hp, ti), 2 * nhp + hp)),
        pl.BlockSpec((2, 2 * c, c), lambda b, hp, ti: (0, 0, 0)),
    ]
    return pl.pallas_call(
        _dilated_attn_kernel,
        grid=(batch, nhp, nt),
        in_specs=in_specs,
        out_specs=pl.BlockSpec(full, lambda b, hp, ti: (row(b, hp, ti), hp)),
        out_shape=jax.ShapeDtypeStruct((batch * seq, WIDTH_C), BF16),
        scratch_shapes=[pltpu.VMEM(full, F32)] * 6,
        compiler_params=_cparams("parallel", "parallel", "arbitrary"),
        name="dilated_attn",
    )(qk, qk, qk, vt, vt, qk4, qk4, qk4, vt4, vt4, qkv16, qkv16, qkv16, qkv16, qkv16, bias)


def _out_mlp_kernel(oa_ref, ob_ref, oc_ref, x_ref, oah_ref, obh_ref, och_ref, xh_ref, wo_ref, gn_ref,
                    wup_ref, cw_ref, cb_ref, wd_ref, gf_ref, o_ref, hcat_ref, u_ref, *, tiles_per_seq, final_norm):
    i = pl.program_id(0)
    tm = x_ref.shape[0]
    tf = FF_TILE
    nf = D_FF // tf
    seq_start = (i % tiles_per_seq) == 0

    def mix_norm(oa, ob, oc, x):
        acc = jnp.dot(oa, wo_ref[:WIDTH_A, :], preferred_element_type=F32)
        acc = acc + jnp.dot(ob, wo_ref[WIDTH_A:WIDTH_A + WIDTH_B, :], preferred_element_type=F32)
        acc = acc + jnp.dot(oc, wo_ref[WIDTH_A + WIDTH_B:, :], preferred_element_type=F32)
        xn = x + acc
        ms = jnp.mean(xn * xn, axis=-1, keepdims=True)
        return xn, (xn * lax.rsqrt(ms + EPS) * gn_ref[...]).astype(BF16)

    _, h_halo = mix_norm(oah_ref[...], obh_ref[...], och_ref[...], xh_ref[...])
    hcat_ref[:HALO, :] = jnp.where(seq_start, jnp.zeros_like(h_halo), h_halo)
    xn, h = mix_norm(oa_ref[...], ob_ref[...], oc_ref[...], x_ref[...])
    hcat_ref[HALO:, :] = h
    o_ref[...] = xn

    def cols(b, c):
        return slice(b * D_FF + c * tf, b * D_FF + (c + 1) * tf)

    def up(c):
        for b in range(2):
            u_ref[c % 2, b] = jnp.dot(hcat_ref[...], wup_ref[:, cols(b, c)], preferred_element_type=F32)

    def down(c):
        def conv(b):
            cw = cw_ref[:, cols(b, c)]
            return (cw[2:3] * u_ref[c % 2, b, HALO:HALO + tm, :]
                    + cw[1:2] * u_ref[c % 2, b, HALO - 1:HALO - 1 + tm, :]
                    + cw[0:1] * u_ref[c % 2, b, HALO - 2:HALO - 2 + tm, :]
                    + cb_ref[:, cols(b, c)])

        gate = conv(0)
        val = conv(1)
        act = (gate * jax.nn.sigmoid(gate) * val).astype(BF16)
        o_ref[...] += jnp.dot(act, wd_ref[c * tf:(c + 1) * tf, :], preferred_element_type=F32)

    up(0)
    for c in range(nf):
        if c + 1 < nf:
            up(c + 1)
        down(c)

    if final_norm:
        y = o_ref[...]
        ms = jnp.mean(y * y, axis=-1, keepdims=True)
        o_ref[...] = y * lax.rsqrt(ms + EPS) * gf_ref[...]


def _out_mlp(oa, ob, oc, x2d, w_out, g_ffn, w_up, conv_w, conv_b, w_down, g_final, seq, final_norm):
    m = x2d.shape[0]
    tm, tf = ROW_TILE, FF_TILE
    hb = tm // HALO
    row = lambda i: (i, 0)
    halo = lambda i: (jnp.maximum(i * hb - 1, 0), 0)
    whole = lambda i: (0, 0)
    once = pl.Buffered(1)
    return pl.pallas_call(
        functools.partial(_out_mlp_kernel, tiles_per_seq=seq // tm, final_norm=final_norm),
        grid=(m // tm,),
        in_specs=[
            pl.BlockSpec((tm, WIDTH_A), row),
            pl.BlockSpec((tm, WIDTH_B), row),
            pl.BlockSpec((tm, WIDTH_C), row),
            pl.BlockSpec((tm, D_MODEL), row),
            pl.BlockSpec((HALO, WIDTH_A), halo),
            pl.BlockSpec((HALO, WIDTH_B), halo),
            pl.BlockSpec((HALO, WIDTH_C), halo),
            pl.BlockSpec((HALO, D_MODEL), halo),
            pl.BlockSpec((D_MODEL, D_MODEL), whole, pipeline_mode=once),
            pl.BlockSpec((1, D_MODEL), whole),
            pl.BlockSpec((D_MODEL, 2 * D_FF), whole, pipeline_mode=once),
            pl.BlockSpec((3, 2 * D_FF), whole, pipeline_mode=once),
            pl.BlockSpec((1, 2 * D_FF), whole, pipeline_mode=once),
            pl.BlockSpec((D_FF, D_MODEL), whole, pipeline_mode=once),
            pl.BlockSpec((1, D_MODEL), whole),
        ],
        out_specs=pl.BlockSpec((tm, D_MODEL), row),
        out_shape=jax.ShapeDtypeStruct((m, D_MODEL), F32),
        scratch_shapes=[
            pltpu.VMEM((HALO + tm, D_MODEL), BF16),
            pltpu.VMEM((2, 2, HALO + tm, tf), F32),
        ],
        compiler_params=_cparams("parallel"),
        name="out_mlp",
    )(oa, ob, oc, x2d, oa, ob, oc, x2d, w_out, g_ffn, w_up, conv_w, conv_b, w_down, g_final)


def _split_w_in(w):
    a, b = WIDTH_A, WIDTH_B
    qk_cols = [w[:, 0:2 * a], w[:, 3 * a:3 * a + 2 * b], w[:, 3 * a + 3 * b:3 * a + 3 * b + 2 * WIDTH_C]]
    v_cols = [w[:, 2 * a:3 * a], w[:, 3 * a + 2 * b:3 * a + 3 * b], w[:, 3 * a + 3 * b + 2 * WIDTH_C:]]
    return (jnp.concatenate(qk_cols, axis=1).astype(BF16),
            jnp.concatenate(v_cols, axis=1).T.astype(BF16))


def kernel(x, positions, g_mix, w_in, w_out, lambda_q1, lambda_k1, lambda_q2, lambda_k2, g_diff, g_ffn,
           w_up, conv_w, conv_b, w_down, g_final):
    batch, seq, d_model = x.shape
    depth = g_mix.shape[0]
    assert d_model == D_MODEL and seq % DIL_TILE == 0 and seq % MOBA_BLOCK == 0
    m = batch * seq
    tab = _rope_tables(positions)
    bias = jnp.asarray(_dilated_band_bias())
    xf = x.reshape(m, d_model)
    for layer in range(depth):
        lam_init = 0.8 - 0.6 * math.exp(-0.3 * layer)
        wqk, wvt = _split_w_in(w_in[layer])
        qk, ksa, ksb, vt, qk4, vt4, qkv16 = _proj_in(xf, g_mix[layer][None, :], wqk, wvt, tab)
        g_t = jnp.broadcast_to(jnp.tile(g_diff[layer], WIDTH_A // HEAD_DIM)[:, None], (WIDTH_A, ATT_BLOCK))
        o_a = _diff_attn(qk, ksa, vt, lambda_q1[layer][None, :], lambda_k1[layer][None, :],
                         lambda_q2[layer][None, :], lambda_k2[layer][None, :], g_t, batch, seq, lam_init)
        o_b = _moba_attn(qk, ksb, vt, batch, seq)
        o_c = _dilated_attn(qk, vt, qk4, vt4, qkv16, bias, batch, seq)
        xf = _out_mlp(o_a, o_b, o_c, xf, w_out[layer].astype(BF16), g_ffn[layer][None, :],
                      w_up[layer].astype(BF16), conv_w[layer], conv_b[layer][None, :],
                      w_down[layer].astype(BF16), g_final[None, :], seq, layer == depth - 1)
    return xf.reshape(batch, seq, d_model)
```

```python
import functools
import math

import numpy as np
import jax
import jax.numpy as jnp
from jax import lax
from jax.experimental import pallas as pl
from jax.experimental.pallas import tpu as pltpu

F32 = jnp.float32
BF16 = jnp.bfloat16

D_MODEL = 1024
HEAD_DIM = 64
DIFF_DIM = 32
WIDTH_A = 256
WIDTH_B = 256
WIDTH_C = 512
ROPE_THETA = 500000.0
MOBA_BLOCK = 256
MOBA_TOPK = 3
DILATED_PATTERNS = ((128, 1), (512, 4), (2048, 16))
D_FF = 2816
EPS = 1e-6
NEG_INF = -1e30
LOG2E = math.log2(math.e)

LANES = 128
ATT_BLOCK = 256
DIL_TILE = 2048
ROW_TILE = 512
FF_TILE = 256
HALO = 16
ONES_ROWS = 16
VMEM_LIMIT = 48 * 1024 * 1024

_QK_SEGMENTS = (
    (0, 256, "qa", ("qk", 0)), (256, 256, "ka", ("ks", 0)),
    (512, 256, "q", ("qk", 256)), (768, 256, "k", ("ks", 1)),
    (1024, 512, "q", ("qk", 512)), (1536, 512, "k", ("qk", 1024)),
)
_W_QK_WIDTH = 2048
_QK_WIDTH = 1536
_V_WIDTH = 1024
_KS_STREAMS = (4, 2)
_QA_BLK, _QB_BLK, _QC_BLK, _KC_BLK = 0, 2, 4, 8
_VA_BLK, _VB_BLK, _VC_BLK = 0, 2, 4
_W_QC_START = 1024


def _cparams(*sem):
    return pltpu.CompilerParams(dimension_semantics=sem, vmem_limit_bytes=VMEM_LIMIT)


def _rope_table_kernel(pos_ref, tab_ref):
    pos = pos_ref[...].astype(F32)
    tr = pos.shape[0]
    lane = lax.broadcasted_iota(jnp.int32, (1, LANES), 1)
    half, rot = HEAD_DIM // 8, HEAD_DIM // 4
    d = lane & (HEAD_DIM - 1)
    j = jnp.where(d < half, d, d - half)
    expo = -(j.astype(F32)) * 2.0 / rot
    inv = jnp.where(d < rot, jnp.power(jnp.float32(ROPE_THETA), expo), 0.0)
    ang = pos * inv
    cos_c, sin_c = jnp.cos(ang), jnp.sin(ang)
    da = lane & (DIFF_DIM - 1)
    half_a = DIFF_DIM // 8
    ja = jnp.where(da < half_a, da, da - half_a)
    src = jnp.where(da < 2 * half_a, half + 2 * ja, rot)
    idx = jnp.broadcast_to(src, (tr, LANES))
    cos_a = jnp.take_along_axis(cos_c, idx, axis=1)
    sin_a = jnp.take_along_axis(sin_c, idx, axis=1)
    tab_ref[:, :LANES] = cos_a
    tab_ref[:, LANES:2 * LANES] = cos_c
    tab_ref[:, 2 * LANES:3 * LANES] = sin_a * jnp.where(da < half_a, -1.0, 1.0).astype(F32)
    tab_ref[:, 3 * LANES:] = sin_c * jnp.where(d < half, -1.0, 1.0).astype(F32)


def _rope_tables(positions):
    m = positions.size
    tr = ROW_TILE
    return pl.pallas_call(
        _rope_table_kernel,
        grid=(m // tr,),
        in_specs=[pl.BlockSpec((tr, 1), lambda i: (i, 0))],
        out_specs=pl.BlockSpec((tr, 4 * LANES), lambda i: (i, 0)),
        out_shape=jax.ShapeDtypeStruct((m, 4 * LANES), F32),
        compiler_params=_cparams("parallel"),
        name="rope_tables",
    )(positions.reshape(m, 1))


def _nt_dot(a, b):
    return lax.dot_general(a, b, (((1,), (1,)), ((), ())), preferred_element_type=F32)


def _proj_in_kernel(x_ref, g_ref, wqk_ref, wvt_ref, tab_ref, qk_ref, ksa_ref, ksb_ref, vt_ref, qk4_ref, vt4_ref,
                    qkv16_ref, cs_ref, c4_ref):
    tm = x_ref.shape[0]
    x = x_ref[...]
    ms = jnp.mean(x * x, axis=-1, keepdims=True)
    h = (x * lax.rsqrt(ms + EPS) * g_ref[...]).astype(BF16)
    lane = lax.broadcasted_iota(jnp.int32, (1, LANES), 1)
    ks_refs = (ksa_ref, ksb_ref)
    slab = 0
    for start, width, kind, dest in _QK_SEGMENTS:
        acc = jnp.dot(h, wqk_ref[:, start:start + width], preferred_element_type=F32)
        narrow = kind.endswith("a")
        dim = DIFF_DIM if narrow else HEAD_DIM
        half = dim // 8
        off = 0 if narrow else LANES
        cos_t = tab_ref[:, off:off + LANES]
        sin_t = tab_ref[:, 2 * LANES + off:3 * LANES + off]
        upper = (lane & (dim - 1)) >= half
        scale = dim ** -0.5 * LOG2E if kind.startswith("q") else None
        for c in range(width // LANES):
            xs = acc[:, c * LANES:(c + 1) * LANES]
            partner = jnp.where(upper, pltpu.roll(xs, half, 1), pltpu.roll(xs, LANES - half, 1))
            y = xs * cos_t + partner * sin_t
            if scale is not None:
                y = y * scale
            if dest[0] == "qk":
                col = dest[1] + c * LANES
                qk_ref[:, col:col + LANES] = y.astype(BF16)
            else:
                n = _KS_STREAMS[dest[1]]
                yb = y.astype(BF16)
                for i in range(n):
                    keep = (lane >= i * dim) & (lane < (i + 1) * dim)
                    col = (c * n + i) * LANES
                    ks_refs[dest[1]][:, col:col + LANES] = jnp.where(keep, yb, jnp.zeros_like(yb))
            if start >= _W_QC_START:
                cs_ref[slab] = y
                slab += 1
    t = ATT_BLOCK
    for r in range(_V_WIDTH // t):
        vt = _nt_dot(wvt_ref[r * t:(r + 1) * t, :], h)
        vtb = vt.astype(BF16)
        for c in range(vt_ref.shape[0]):
            vt_ref[c, r * t:(r + 1) * t, :] = vtb[:, c * t:(c + 1) * t]
        if r * t >= _VC_BLK * LANES:
            v_nat = vt.T
            for c in range(t // LANES):
                cs_ref[slab] = v_nat[:, c * LANES:(c + 1) * LANES]
                slab += 1
    n_qk = 2 * WIDTH_C // LANES
    n_all = 3 * WIDTH_C // LANES
    assert slab == n_all
    quarter = tm // 4
    for s in range(n_all):
        rows = [cs_ref[s, pl.ds(r, quarter, stride=4), :] for r in range(4)]
        for r in range(4):
            c4_ref[s, r * quarter:(r + 1) * quarter, :] = rows[r]
        if s < n_qk:
            for r in range(4):
                qk4_ref[r * quarter:(r + 1) * quarter, s * LANES:(s + 1) * LANES] = rows[r].astype(BF16)
        else:
            v4t = jnp.concatenate(rows, axis=0).T.astype(BF16)
            vs = s - n_qk
            for c in range(vt4_ref.shape[0]):
                vt4_ref[c, vs * LANES:(vs + 1) * LANES, :] = v4t[:, c * t:(c + 1) * t]
    per = tm // 16
    for s in range(n_all):
        for r1 in range(4):
            for r2 in range(4):
                qkv16_ref[0, 4 * r2 + r1, 0, :, s * LANES:(s + 1) * LANES] = (
                    c4_ref[s, pl.ds(r1 * quarter + r2, per, stride=4), :].astype(BF16))


def _proj_in(x2d, g, wqk, wvt, tab):
    m = x2d.shape[0]
    tm, t = ROW_TILE, ATT_BLOCK
    assert tm == DIL_TILE // 4 and tm == 4 * LANES
    per = tm // 16
    n_all = 3 * WIDTH_C // LANES
    outs = pl.pallas_call(
        _proj_in_kernel,
        grid=(m // tm,),
        in_specs=[
            pl.BlockSpec((tm, D_MODEL), lambda i: (i, 0)),
            pl.BlockSpec((1, D_MODEL), lambda i: (0, 0)),
            pl.BlockSpec((D_MODEL, _W_QK_WIDTH), lambda i: (0, 0)),
            pl.BlockSpec((_V_WIDTH, D_MODEL), lambda i: (0, 0)),
            pl.BlockSpec((tm, 4 * LANES), lambda i: (i, 0)),
        ],
        out_specs=[
            pl.BlockSpec((tm, _QK_WIDTH), lambda i: (i, 0)),
            pl.BlockSpec((tm, WIDTH_A * _KS_STREAMS[0]), lambda i: (i, 0)),
            pl.BlockSpec((tm, WIDTH_B * _KS_STREAMS[1]), lambda i: (i, 0)),
            pl.BlockSpec((tm // t, _V_WIDTH, t), lambda i: (i, 0, 0)),
            pl.BlockSpec((tm, 2 * WIDTH_C), lambda i: (i, 0)),
            pl.BlockSpec((tm // t, WIDTH_C, t), lambda i: (i, 0, 0)),
            pl.BlockSpec((1, 16, 1, per, 3 * WIDTH_C), lambda i: (i // 4, 0, i % 4, 0, 0)),
        ],
        out_shape=[
            jax.ShapeDtypeStruct((m, _QK_WIDTH), BF16),
            jax.ShapeDtypeStruct((m, WIDTH_A * _KS_STREAMS[0]), BF16),
            jax.ShapeDtypeStruct((m, WIDTH_B * _KS_STREAMS[1]), BF16),
            jax.ShapeDtypeStruct((m // t, _V_WIDTH, t), BF16),
            jax.ShapeDtypeStruct((m, 2 * WIDTH_C), BF16),
            jax.ShapeDtypeStruct((m // t, WIDTH_C, t), BF16),
            jax.ShapeDtypeStruct((m // DIL_TILE, 16, 4, per, 3 * WIDTH_C), BF16),
        ],
        scratch_shapes=[pltpu.VMEM((n_all, tm, LANES), F32)] * 2,
        compiler_params=_cparams("parallel"),
        name="proj_in",
    )(x2d, g, wqk, wvt, tab)
    qk, ksa, ksb, vt, qk4, vt4, qkv16 = outs
    return qk, ksa, ksb, vt, qk4, vt4, qkv16.reshape(m, 3 * WIDTH_C)


def _with_ones_rows(vt):
    return jnp.concatenate([vt, jnp.ones((ONES_ROWS, vt.shape[1]), vt.dtype)], axis=0)


def _softmax_first(ss, vts, m_ref, l_ref, acc_ref):
    ms = [jnp.max(s, axis=0, keepdims=True) for s in ss]
    ps = [jnp.exp2(s - m).astype(BF16) for s, m in zip(ss, ms)]
    for i, (p, m, vt) in enumerate(zip(ps, ms, vts)):
        r = jnp.dot(_with_ones_rows(vt), p, preferred_element_type=F32)
        m_ref[i] = m
        l_ref[i] = r[HEAD_DIM:HEAD_DIM + 1]
        acc_ref[i] = r[:HEAD_DIM]


def _softmax_next(ss, vts, m_ref, l_ref, acc_ref, col_biases=None):
    n = len(ss)
    m_prevs = [m_ref[i] for i in range(n)]
    m_blks = [jnp.max(s, axis=0, keepdims=True) for s in ss]
    if col_biases is not None:
        m_blks = [m + b for m, b in zip(m_blks, col_biases)]
    m_news = [jnp.maximum(a, b) for a, b in zip(m_prevs, m_blks)]
    shifts = m_news if col_biases is None else [m - b for m, b in zip(m_news, col_biases)]
    ps = [jnp.exp2(s - sh).astype(BF16) for s, sh in zip(ss, shifts)]
    for i in range(n):
        alpha = jnp.exp2(m_prevs[i] - m_news[i])
        r = jnp.dot(_with_ones_rows(vts[i]), ps[i], preferred_element_type=F32)
        m_ref[i] = m_news[i]
        l_ref[i] = alpha * l_ref[i] + r[HEAD_DIM:HEAD_DIM + 1]
        acc_ref[i] = alpha * acc_ref[i] + r[:HEAD_DIM]


def _causal_mask_t(t):
    key = lax.broadcasted_iota(jnp.int32, (t, t), 0)
    qry = lax.broadcasted_iota(jnp.int32, (t, t), 1)
    return key <= qry


def _masked_stack(k_blk, lane, width, n):
    zero = jnp.zeros_like(k_blk)
    return jnp.concatenate(
        [jnp.where((lane >= i * width) & (lane < (i + 1) * width), k_blk, zero) for i in range(n)], axis=0)


def _ks_block(ks_ref, j, n):
    blk = ks_ref[pl.ds(pl.multiple_of(j * ATT_BLOCK, ATT_BLOCK), ATT_BLOCK), :]
    return jnp.concatenate([blk[:, i * LANES:(i + 1) * LANES] for i in range(n)], axis=0)


def _att_scratch(n_streams):
    t = ATT_BLOCK
    return [
        pltpu.VMEM((n_streams, 1, t), F32),
        pltpu.VMEM((n_streams, 1, t), F32),
        pltpu.VMEM((n_streams, HEAD_DIM, t), F32),
        pltpu.VMEM((n_streams * t, t), F32),
        pltpu.VMEM((n_streams * t, t), F32),
    ]


def _pipelined_kv_loop(lo, hi, scores, s_a, s_b, step):
    n = hi - lo

    def body(i, carry):
        j = lo + 2 * i
        s_b[...] = scores(j + 1)
        step(j, s_a)
        s_a[...] = scores(j + 2)
        step(j + 1, s_b)
        return carry

    lax.fori_loop(0, n // 2, body, 0)

    @pl.when(n % 2 == 1)
    def _():
        step(hi - 1, s_a)


def _group_specs(q_col, width, n_copies, vt_row, seq):
    t = ATT_BLOCK
    nq = seq // t
    assert q_col % width == 0 and vt_row % width == 0
    return [
        pl.BlockSpec((t, width), lambda b, qi: (b * nq + qi, q_col // width)),
        pl.BlockSpec((seq, n_copies * LANES), lambda b, qi: (b, 0)),
        pl.BlockSpec((nq, width, t), lambda b, qi: (b, vt_row // width, 0)),
    ]


def _pair_scores(stacked, qs):
    rows = stacked.shape[0] // len(qs)
    return jnp.concatenate([_nt_dot(stacked[hp * rows:(hp + 1) * rows], qs[hp]) for hp in range(len(qs))], axis=0)


def _diff_attn_kernel(q_ref, k_ref, vt_ref, lq1_ref, lk1_ref, lq2_ref, lk2_ref, g_ref, o_ref,
                      m_ref, l_ref, acc_ref, s_a, s_b, *, lam_init):
    qi = pl.program_id(1)
    t = ATT_BLOCK
    nh = WIDTH_A // HEAD_DIM
    ns = 2 * nh
    q = q_ref[...]
    qs = [q[:, hp * LANES:(hp + 1) * LANES] for hp in range(WIDTH_A // LANES)]

    def scores(j):
        return _pair_scores(_ks_block(k_ref, j, ns), qs)

    def values(j):
        vt = vt_ref[j]
        return [vt[(i // 2) * HEAD_DIM:(i // 2 + 1) * HEAD_DIM] for i in range(ns)]

    s_all = scores(qi)
    s_a[...] = scores(0)
    mask = _causal_mask_t(t)
    _softmax_first([jnp.where(mask, s_all[i * t:(i + 1) * t], NEG_INF) for i in range(ns)], values(qi),
                   m_ref, l_ref, acc_ref)

    def step(j, s_slot):
        _softmax_next([s_slot[i * t:(i + 1) * t, :] for i in range(ns)], values(j), m_ref, l_ref, acc_ref)

    _pipelined_kv_loop(0, qi, scores, s_a, s_b, step)

    lam = (jnp.exp(jnp.sum(lq1_ref[...] * lk1_ref[...], axis=1, keepdims=True))
           - jnp.exp(jnp.sum(lq2_ref[...] * lk2_ref[...], axis=1, keepdims=True)) + lam_init)
    outs = []
    for hh in range(nh):
        o = acc_ref[2 * hh] / l_ref[2 * hh] - lam * (acc_ref[2 * hh + 1] / l_ref[2 * hh + 1])
        ms = jnp.mean(o * o, axis=0, keepdims=True)
        outs.append(o * lax.rsqrt(ms + EPS))
    y = jnp.concatenate(outs, axis=0) * g_ref[...] * (1.0 - lam_init)
    o_ref[...] = y.T.astype(BF16)


def _diff_attn(qk, ksa, vt, lq1, lk1, lq2, lk2, g_t, batch, seq, lam_init):
    t = ATT_BLOCK
    nq = seq // t
    ns = 2 * WIDTH_A // HEAD_DIM
    vec = pl.BlockSpec((1, DIFF_DIM), lambda b, qi: (0, 0))
    return pl.pallas_call(
        functools.partial(_diff_attn_kernel, lam_init=lam_init),
        grid=(batch, nq),
        in_specs=_group_specs(_QA_BLK * LANES, WIDTH_A, ns, _VA_BLK * LANES, seq) + [
            vec, vec, vec, vec,
            pl.BlockSpec((WIDTH_A, t), lambda b, qi: (0, 0)),
        ],
        out_specs=pl.BlockSpec((t, WIDTH_A), lambda b, qi: (b * nq + qi, 0)),
        out_shape=jax.ShapeDtypeStruct((batch * seq, WIDTH_A), BF16),
        scratch_shapes=_att_scratch(ns),
        compiler_params=_cparams("parallel", "arbitrary"),
        name="diff_attn",
    )(qk, ksa, vt, lq1, lk1, lq2, lk2, g_t)


def _moba_attn_kernel(q_ref, k_ref, vt_ref, o_ref, kmean_ref, bias_ref, m_ref, l_ref, acc_ref, s_a, s_b, *, nb):
    qi = pl.program_id(1)
    t = ATT_BLOCK
    nh = WIDTH_B // HEAD_DIM

    @pl.when(qi == 0)
    def _():
        kmean_ref[...] = jnp.zeros_like(kmean_ref)
        for j in range(nb):
            kb = k_ref[j * t:(j + 1) * t, :].astype(F32)
            km = jnp.mean(kb, axis=0, keepdims=True)
            for i in range(nh):
                kmean_ref[i * LANES + j:i * LANES + j + 1, :] = km[:, i * LANES:(i + 1) * LANES]

    q = q_ref[...]
    qs = [q[:, hp * LANES:(hp + 1) * LANES] for hp in range(WIDTH_B // LANES)]
    gate_all = _pair_scores(kmean_ref[...].astype(BF16), qs)
    blk = lax.broadcasted_iota(jnp.int32, (nb, t), 0)
    for i in range(nh):
        gate = gate_all[i * LANES:i * LANES + nb]
        cnt = jnp.zeros((nb, t), jnp.int32)
        for r in range(nb):
            gr = gate[r:r + 1, :]
            beats = (gr > gate) | ((gr == gate) & (blk > r))
            live = jnp.where(r < qi, 1, 0)
            cnt = cnt + jnp.where(beats, live, 0)
        sel = (cnt < MOBA_TOPK) & (blk < qi)
        bias_ref[i] = jnp.where(sel, 0.0, NEG_INF)

    def scores(j):
        return _pair_scores(_ks_block(k_ref, j, nh), qs)

    def values(j):
        vt = vt_ref[j]
        return [vt[i * HEAD_DIM:(i + 1) * HEAD_DIM] for i in range(nh)]

    s_all = scores(qi)
    s_a[...] = scores(0)
    mask = _causal_mask_t(t)
    _softmax_first([jnp.where(mask, s_all[i * t:(i + 1) * t], NEG_INF) for i in range(nh)], values(qi),
                   m_ref, l_ref, acc_ref)

    def step(j, s_slot):
        _softmax_next([s_slot[i * t:(i + 1) * t, :] for i in range(nh)], values(j), m_ref, l_ref, acc_ref,
                      col_biases=[bias_ref[i, pl.ds(j, 1), :] for i in range(nh)])

    _pipelined_kv_loop(0, qi, scores, s_a, s_b, step)

    o = jnp.concatenate([acc_ref[i] / l_ref[i] for i in range(nh)], axis=0)
    o_ref[...] = o.T.astype(BF16)


def _moba_attn(qk, ksb, vt, batch, seq):
    t = ATT_BLOCK
    nq = seq // t
    nb = seq // MOBA_BLOCK
    nh = WIDTH_B // HEAD_DIM
    return pl.pallas_call(
        functools.partial(_moba_attn_kernel, nb=nb),
        grid=(batch, nq),
        in_specs=_group_specs(_QB_BLK * LANES, WIDTH_B, nh, _VB_BLK * LANES, seq),
        out_specs=pl.BlockSpec((t, WIDTH_B), lambda b, qi: (b * nq + qi, 0)),
        out_shape=jax.ShapeDtypeStruct((batch * seq, WIDTH_B), BF16),
        scratch_shapes=[pltpu.VMEM((nh * LANES, LANES), F32), pltpu.VMEM((nh, nb, t), F32)] + _att_scratch(nh),
        compiler_params=_cparams("parallel", "arbitrary"),
        name="moba_attn",
    )(qk, ksb, vt)


def _dilated_band_bias():
    c = LANES
    key = np.arange(2 * c)[:, None]
    qry = np.arange(c)[None, :]
    valid = np.where(key < c, qry <= key, key - c <= qry)
    no_prev = valid & (key >= c)
    return np.where(np.stack([valid, no_prev]), 0.0, NEG_INF).astype(np.float32)


def _dilated_units(units, lane):
    n = 2 * LANES
    s_alls = [_nt_dot(_masked_stack(jnp.concatenate([kp, ko], axis=0), lane, HEAD_DIM, 2), q)
              for q, kp, ko, _, _, _ in units]
    stats = []
    for s_all, (_, _, _, _, _, bias) in zip(s_alls, units):
        for hh in range(2):
            s = s_all[hh * n:(hh + 1) * n] + bias
            m = jnp.max(s, axis=0, keepdims=True)
            stats.append((m, jnp.exp2(s - m).astype(BF16)))
    results = []
    for i, (_, _, _, vtp, vto, _) in enumerate(units):
        vt_u = jnp.concatenate([vtp, vto], axis=1)
        outs, lses = [], []
        for hh in range(2):
            m, p = stats[2 * i + hh]
            r = jnp.dot(_with_ones_rows(vt_u[hh * HEAD_DIM:(hh + 1) * HEAD_DIM]), p, preferred_element_type=F32)
            acc, l = r[:HEAD_DIM], r[HEAD_DIM:HEAD_DIM + 1]
            outs.append(acc / l)
            lses.append(jnp.broadcast_to(m + jnp.log2(l), acc.shape))
        results.append((jnp.concatenate(outs, axis=0).T, jnp.concatenate(lses, axis=0).T))
    return results


def _dilated_attn_kernel(q1_ref, k1_ref, k1p_ref, vt1_ref, vt1p_ref,
                         q4_ref, k4_ref, k4p_ref, vt4_ref, vt4p_ref,
                         q16_ref, k16_ref, k16p_ref, v16_ref, v16p_ref, bias_ref, o_ref,
                         o1_ref, l1_ref, o4_ref, l4_ref, o16_ref, l16_ref):
    c = LANES
    lane = lax.broadcasted_iota(jnp.int32, (1, LANES), 1)
    first_tile = jnp.where(pl.program_id(2) == 0, 1, 0)
    bias = bias_ref[0]
    bias_prev_tile = bias_ref[first_tile]

    def rows(ref, chunk):
        return ref[pl.ds(pl.multiple_of(chunk * c, c), c), :]

    def vt_chunk(ref, blk, half):
        return ref[blk, :, half * c:(half + 1) * c]

    def unit1(g, u):
        ch = 4 * g + u
        vt_prev = vt_chunk(vt1_ref, 2 * g - 1, 1) if u == 0 else vt_chunk(vt1_ref, 2 * g + (u - 1) // 2, (u - 1) % 2)
        return (rows(q1_ref, ch), rows(k1_ref, ch - 1), rows(k1_ref, ch), vt_prev,
                vt_chunk(vt1_ref, 2 * g + u // 2, u % 2), bias)

    def store1(g, u, o, lse):
        sl = pl.ds(pl.multiple_of((4 * g + u) * c, c), c)
        o1_ref[sl, :] = o
        l1_ref[sl, :] = lse

    def unit4(g, u):
        ch = 4 * g + u
        return (rows(q4_ref, ch), rows(k4_ref, ch - 4), rows(k4_ref, ch),
                vt_chunk(vt4_ref, 2 * (g - 1) + u // 2, u % 2), vt_chunk(vt4_ref, 2 * g + u // 2, u % 2), bias)

    def store4(g, u, o, lse):
        o4_ref[pl.ds(g * 4 * c + u, c, stride=4), :] = o
        l4_ref[pl.ds(g * 4 * c + u, c, stride=4), :] = lse

    def unit16(g, u):
        ch = 4 * g + u
        return (rows(q16_ref, ch), rows(k16p_ref, ch), rows(k16_ref, ch),
                rows(v16p_ref, ch).astype(F32).T.astype(BF16), rows(v16_ref, ch).astype(F32).T.astype(BF16),
                bias_prev_tile)

    def store16(g, u, o, lse):
        o16_ref[pl.ds(4 * g + u, c, stride=16), :] = o
        l16_ref[pl.ds(4 * g + u, c, stride=16), :] = lse

    first = [(q1_ref[:c, :], k1p_ref[...], k1_ref[:c, :], vt1p_ref[0, :, c:], vt1_ref[0, :, :c], bias_prev_tile)]
    first += [unit1(0, u) for u in range(1, 4)]
    first += [(q4_ref[u * c:(u + 1) * c, :], k4p_ref[u * c:(u + 1) * c, :], k4_ref[u * c:(u + 1) * c, :],
               vt4p_ref[u // 2, :, (u % 2) * c:(u % 2 + 1) * c], vt4_ref[u // 2, :, (u % 2) * c:(u % 2 + 1) * c],
               bias_prev_tile) for u in range(4)]
    res = _dilated_units(first, lane)
    for u in range(4):
        store1(0, u, *res[u])
        store4(0, u, *res[4 + u])

    def group(g, carry):
        res = _dilated_units([unit1(g, u) for u in range(4)] + [unit4(g, u) for u in range(4)], lane)
        for u in range(4):
            store1(g, u, *res[u])
            store4(g, u, *res[4 + u])
        return carry

    lax.fori_loop(1, 4, group, 0)

    def group16(g, carry):
        res = _dilated_units([unit16(2 * g + h, u) for h in range(2) for u in range(4)], lane)
        for h in range(2):
            for u in range(4):
                store16(2 * g + h, u, *res[4 * h + u])
        return carry

    lax.fori_loop(0, 2, group16, 0)

    def merge(i, carry):
        sl = pl.ds(pl.multiple_of(i * c, c), c)
        la, lb, lc = l1_ref[sl, :], l4_ref[sl, :], l16_ref[sl, :]
        top = jnp.maximum(jnp.maximum(la, lb), lc)
        wa, wb, wc = jnp.exp2(la - top), jnp.exp2(lb - top), jnp.exp2(lc - top)
        num = wa * o1_ref[sl, :] + wb * o4_ref[sl, :] + wc * o16_ref[sl, :]
        o_ref[sl, :] = (num / (wa + wb + wc)).astype(BF16)
        return carry

    lax.fori_loop(0, DIL_TILE // c, merge, 0)


def _dilated_attn(qk, vt, qk4, vt4, qkv16, bias, batch, seq):
    tile = DIL_TILE
    nt = seq // tile
    c = LANES
    t = ATT_BLOCK
    nhp = WIDTH_C // LANES
    row = lambda b, hp, ti: b * nt + ti

    def prev(blocks_per_tile):
        return lambda b, hp, ti: jnp.maximum((b * nt + ti) * blocks_per_tile - 1, 0)

    p16, p8, p4, p1 = prev(tile // c), prev(tile // t), prev(4), prev(1)
    full = (tile, LANES)
    vfull = (tile // t, LANES, t)
    in_specs = [
        pl.BlockSpec(full, lambda b, hp, ti: (row(b, hp, ti), _QC_BLK + hp)),
        pl.BlockSpec(full, lambda b, hp, ti: (row(b, hp, ti), _KC_BLK + hp)),
        pl.BlockSpec((c, LANES), lambda b, hp, ti: (p16(b, hp, ti), _KC_BLK + hp)),
        pl.BlockSpec(vfull, lambda b, hp, ti: (row(b, hp, ti), _VC_BLK + hp, 0)),
        pl.BlockSpec((1, LANES, t), lambda b, hp, ti: (p8(b, hp, ti), _VC_BLK + hp, 0)),
        pl.BlockSpec(full, lambda b, hp, ti: (row(b, hp, ti), hp)),
        pl.BlockSpec(full, lambda b, hp, ti: (row(b, hp, ti), nhp + hp)),
        pl.BlockSpec((4 * c, LANES), lambda b, hp, ti: (p4(b, hp, ti), nhp + hp)),
        pl.BlockSpec(vfull, lambda b, hp, ti: (row(b, hp, ti), hp, 0)),
        pl.BlockSpec((2, LANES, t), lambda b, hp, ti: (p4(b, hp, ti), hp, 0)),
        pl.BlockSpec(full, lambda b, hp, ti: (row(b, hp, ti), hp)),
        pl.BlockSpec(full, lambda b, hp, ti: (row(b, hp, ti), nhp + hp)),
        pl.BlockSpec(full, lambda b, hp, ti: (p1(b, hp, ti), nhp + hp)),
        pl.BlockSpec(full, lambda b, hp, ti: (row(b, hp, ti), 2 * nhp + hp)),
        pl.BlockSpec(full, lambda b, hp, ti: (p1(b, hp, ti), 2 * nhp + hp)),
        pl.BlockSpec((2, 2 * c, c), lambda b, hp, ti: (0, 0, 0)),
    ]
    return pl.pallas_call(
        _dilated_attn_kernel,
        grid=(batch, nhp, nt),
        in_specs=in_specs,
        out_specs=pl.BlockSpec(full, lambda b, hp, ti: (row(b, hp, ti), hp)),
        out_shape=jax.ShapeDtypeStruct((batch * seq, WIDTH_C), BF16),
        scratch_shapes=[pltpu.VMEM(full, F32)] * 6,
        compiler_params=_cparams("parallel", "parallel", "arbitrary"),
        name="dilated_attn",
    )(qk, qk, qk, vt, vt, qk4, qk4, qk4, vt4, vt4, qkv16, qkv16, qkv16, qkv16, qkv16, bias)


def _out_mlp_kernel(oa_ref, ob_ref, oc_ref, x_ref, oah_ref, obh_ref, och_ref, xh_ref, wo_ref, gn_ref,
                    wup_ref, cw_ref, cb_ref, wd_ref, gf_ref, o_ref, hcat_ref, u_ref, *, tiles_per_seq, final_norm):
    i = pl.program_id(0)
    tm = x_ref.shape[0]
    tf = FF_TILE
    nf = D_FF // tf
    seq_start = (i % tiles_per_seq) == 0

    def mix_norm(oa, ob, oc, x):
        acc = jnp.dot(oa, wo_ref[:WIDTH_A, :], preferred_element_type=F32)
        acc = acc + jnp.dot(ob, wo_ref[WIDTH_A:WIDTH_A + WIDTH_B, :], preferred_element_type=F32)
        acc = acc + jnp.dot(oc, wo_ref[WIDTH_A + WIDTH_B:, :], preferred_element_type=F32)
        xn = x + acc
        ms = jnp.mean(xn * xn, axis=-1, keepdims=True)
        return xn, (xn * lax.rsqrt(ms + EPS) * gn_ref[...]).astype(BF16)

    def with_halo(halo_ref, ref):
        return jnp.concatenate([halo_ref[...], ref[...]], axis=0)

    xn, h = mix_norm(with_halo(oah_ref, oa_ref), with_halo(obh_ref, ob_ref), with_halo(och_ref, oc_ref),
                     with_halo(xh_ref, x_ref))
    hcat_ref[...] = h
    hcat_ref[:HALO, :] = jnp.where(seq_start, jnp.zeros_like(h[:HALO]), h[:HALO])
    o_ref[...] = xn[HALO:]

    def cols(b, c):
        return slice(b * D_FF + c * tf, b * D_FF + (c + 1) * tf)

    def up(c):
        for b in range(2):
            u_ref[c % 2, b] = jnp.dot(hcat_ref[...], wup_ref[:, cols(b, c)], preferred_element_type=F32)

    def down(c):
        def conv(b):
            cw = cw_ref[:, cols(b, c)]
            return (cw[2:3] * u_ref[c % 2, b, HALO:HALO + tm, :]
                    + cw[1:2] * u_ref[c % 2, b, HALO - 1:HALO - 1 + tm, :]
                    + cw[0:1] * u_ref[c % 2, b, HALO - 2:HALO - 2 + tm, :]
                    + cb_ref[:, cols(b, c)])

        gate = conv(0)
        val = conv(1)
        act = (gate * jax.nn.sigmoid(gate) * val).astype(BF16)
        o_ref[...] += jnp.dot(act, wd_ref[c * tf:(c + 1) * tf, :], preferred_element_type=F32)

    up(0)
    for c in range(nf):
        if c + 1 < nf:
            up(c + 1)
        down(c)

    if final_norm:
        y = o_ref[...]
        ms = jnp.mean(y * y, axis=-1, keepdims=True)
        o_ref[...] = y * lax.rsqrt(ms + EPS) * gf_ref[...]


def _out_mlp(oa, ob, oc, x2d, w_out, g_ffn, w_up, conv_w, conv_b, w_down, g_final, seq, final_norm):
    m = x2d.shape[0]
    tm, tf = ROW_TILE, FF_TILE
    hb = tm // HALO
    row = lambda i: (i, 0)
    halo = lambda i: (jnp.maximum(i * hb - 1, 0), 0)
    whole = lambda i: (0, 0)
    once = pl.Buffered(1)
    return pl.pallas_call(
        functools.partial(_out_mlp_kernel, tiles_per_seq=seq // tm, final_norm=final_norm),
        grid=(m // tm,),
        in_specs=[
            pl.BlockSpec((tm, WIDTH_A), row),
            pl.BlockSpec((tm, WIDTH_B), row),
            pl.BlockSpec((tm, WIDTH_C), row),
            pl.BlockSpec((tm, D_MODEL), row),
            pl.BlockSpec((HALO, WIDTH_A), halo),
            pl.BlockSpec((HALO, WIDTH_B), halo),
            pl.BlockSpec((HALO, WIDTH_C), halo),
            pl.BlockSpec((HALO, D_MODEL), halo),
            pl.BlockSpec((D_MODEL, D_MODEL), whole, pipeline_mode=once),
            pl.BlockSpec((1, D_MODEL), whole),
            pl.BlockSpec((D_MODEL, 2 * D_FF), whole, pipeline_mode=once),
            pl.BlockSpec((3, 2 * D_FF), whole, pipeline_mode=once),
            pl.BlockSpec((1, 2 * D_FF), whole, pipeline_mode=once),
            pl.BlockSpec((D_FF, D_MODEL), whole, pipeline_mode=once),
            pl.BlockSpec((1, D_MODEL), whole),
        ],
        out_specs=pl.BlockSpec((tm, D_MODEL), row),
        out_shape=jax.ShapeDtypeStruct((m, D_MODEL), F32),
        scratch_shapes=[
            pltpu.VMEM((HALO + tm, D_MODEL), BF16),
            pltpu.VMEM((2, 2, HALO + tm, tf), F32),
        ],
        compiler_params=_cparams("parallel"),
        name="out_mlp",
    )(oa, ob, oc, x2d, oa, ob, oc, x2d, w_out, g_ffn, w_up, conv_w, conv_b, w_down, g_final)


def _split_w_in(w):
    a, b = WIDTH_A, WIDTH_B
    qk_cols = [w[:, 0:2 * a], w[:, 3 * a:3 * a + 2 * b], w[:, 3 * a + 3 * b:3 * a + 3 * b + 2 * WIDTH_C]]
    v_cols = [w[:, 2 * a:3 * a], w[:, 3 * a + 2 * b:3 * a + 3 * b], w[:, 3 * a + 3 * b + 2 * WIDTH_C:]]
    return (jnp.concatenate(qk_cols, axis=1).astype(BF16),
            jnp.concatenate(v_cols, axis=1).T.astype(BF16))


def kernel(x, positions, g_mix, w_in, w_out, lambda_q1, lambda_k1, lambda_q2, lambda_k2, g_diff, g_ffn,
           w_up, conv_w, conv_b, w_down, g_final):
    batch, seq, d_model = x.shape
    depth = g_mix.shape[0]
    assert d_model == D_MODEL and seq % DIL_TILE == 0 and seq % MOBA_BLOCK == 0
    m = batch * seq
    tab = _rope_tables(positions)
    bias = jnp.asarray(_dilated_band_bias())
    xf = x.reshape(m, d_model)
    for layer in range(depth):
        lam_init = 0.8 - 0.6 * math.exp(-0.3 * layer)
        wqk, wvt = _split_w_in(w_in[layer])
        qk, ksa, ksb, vt, qk4, vt4, qkv16 = _proj_in(xf, g_mix[layer][None, :], wqk, wvt, tab)
        g_t = jnp.broadcast_to(jnp.tile(g_diff[layer], WIDTH_A // HEAD_DIM)[:, None], (WIDTH_A, ATT_BLOCK))
        o_a = _diff_attn(qk, ksa, vt, lambda_q1[layer][None, :], lambda_k1[layer][None, :],
                         lambda_q2[layer][None, :], lambda_k2[layer][None, :], g_t, batch, seq, lam_init)
        o_b = _moba_attn(qk, ksb, vt, batch, seq)
        o_c = _dilated_attn(qk, vt, qk4, vt4, qkv16, bias, batch, seq)
        xf = _out_mlp(o_a, o_b, o_c, xf, w_out[layer].astype(BF16), g_ffn[layer][None, :],
                      w_up[layer].astype(BF16), conv_w[layer], conv_b[layer][None, :],
                      w_down[layer].astype(BF16), g_final[None, :], seq, layer == depth - 1)
    return xf.reshape(batch, seq, d_model)
```
